```python
import math
import jax, jax.numpy as jnp
from jax import lax
import numpy as np

D_MODEL = 2048
BATCH = 4
SEQ = 4096
DEPTH = 4

N_REC = DEPTH // 2
N_ATTN = DEPTH - N_REC

LRU_WIDTH = D_MODEL
LRU_BLOCKS = 8
LRU_BLOCK_W = LRU_WIDTH // LRU_BLOCKS
CONV_WIDTH = 4
LRU_C = 8.0

HEAD_DIM = 64
N_HEADS = D_MODEL // HEAD_DIM
N_KV_HEADS = N_HEADS // 8
GROUP = N_HEADS // N_KV_HEADS
WINDOW = 128
BLOCK = 128

D_FF = ((8 * D_MODEL // 3 + 255) // 256) * 256

NORM_EPS = 1e-6
NEG_INF = -1e30

kernel_name = "yoco_rglru_swa_sink_hybrid"


def rms_norm(x, g):
    x32 = x.astype(jnp.float32)
    y = x32 * lax.rsqrt(jnp.mean(x32 * x32, axis=-1, keepdims=True) + NORM_EPS)
    return (y * g.astype(jnp.float32)).astype(x.dtype)


def swiglu_ffn(x, g, w_in, w_out):
    h = rms_norm(x, g)
    gate, up = jnp.split(h @ w_in, 2, axis=-1)
    return x + (jax.nn.silu(gate) * up) @ w_out


def causal_depthwise_conv(x, w, b):
    c = x.shape[-1]
    y = lax.conv_general_dilated(
        x, w[:, None, :], window_strides=(1,), padding=[(CONV_WIDTH - 1, 0)],
        dimension_numbers=("NWC", "WIO", "NWC"), feature_group_count=c)
    return y + b


def block_diag_linear(x, w, b):
    bsz, s, wd = x.shape
    xg = x.reshape(bsz, s, LRU_BLOCKS, LRU_BLOCK_W)
    y = jnp.einsum("bsnc,ncd->bsnd", xg, w).reshape(bsz, s, wd)
    return y + b


def rg_lru(xb, w_rg, b_rg, w_ig, b_ig, lam):
    r = jax.nn.sigmoid(block_diag_linear(xb, w_rg, b_rg).astype(jnp.float32))
    i = jax.nn.sigmoid(block_diag_linear(xb, w_ig, b_ig).astype(jnp.float32))
    log_a = -LRU_C * r * jax.nn.softplus(-lam.astype(jnp.float32))
    a = jnp.exp(log_a)
    u = jnp.sqrt(-jnp.expm1(2.0 * log_a)) * (i * xb.astype(jnp.float32))

    def combine(left, right):
        a_l, u_l = left
        a_r, u_r = right
        return a_l * a_r, a_r * u_l + u_r

    _, h = lax.associative_scan(combine, (a, u), axis=1)
    return h.astype(xb.dtype)


def recurrent_block(x, g, w_in, conv_w, conv_b, w_rg, b_rg, w_ig, b_ig, lam, w_out):
    h = rms_norm(x, g)
    y_branch, x_branch = jnp.split(h @ w_in, 2, axis=-1)
    xb = causal_depthwise_conv(x_branch, conv_w, conv_b)
    hs = rg_lru(xb, w_rg, b_rg, w_ig, b_ig, lam)
    return x + (hs * jax.nn.gelu(y_branch)) @ w_out


def shared_kv(x, g, w_kv, k_g):
    bsz, s, _ = x.shape
    nblk = s // BLOCK
    h = rms_norm(x, g)
    k, v = jnp.split(h @ w_kv, 2, axis=-1)
    k = rms_norm(k.reshape(bsz, s, N_KV_HEADS, HEAD_DIM), k_g)
    v = v.reshape(bsz, s, N_KV_HEADS, HEAD_DIM)

    def band(t):
        t = t.reshape(bsz, nblk, BLOCK, N_KV_HEADS, HEAD_DIM)
        prev = jnp.concatenate([jnp.zeros_like(t[:, :1]), t[:, :-1]], axis=1)
        return jnp.concatenate([prev, t], axis=2)

    return band(k), band(v)


def window_mask(nblk):
    q_pos = jnp.arange(BLOCK)[:, None] + BLOCK
    k_pos = jnp.arange(2 * BLOCK)[None, :]
    rel = q_pos - k_pos
    band_ok = (rel >= 0) & (rel < WINDOW)
    exists = (jnp.arange(nblk)[:, None, None] > 0) | (k_pos[None] >= BLOCK)
    return band_ok[None] & exists


def attn_block(x, g, w_q, q_g, sinks, w_o, k_band, v_band):
    bsz, s, _ = x.shape
    nblk = s // BLOCK
    h = rms_norm(x, g)
    q = rms_norm((h @ w_q).reshape(bsz, s, N_KV_HEADS, GROUP, HEAD_DIM), q_g)
    q = q.reshape(bsz, nblk, BLOCK, N_KV_HEADS, GROUP, HEAD_DIM)
    scores = jnp.einsum("bnqhgd,bnkhd->bnhgqk", q, k_band,
                        preferred_element_type=jnp.float32) * (1.0 / math.sqrt(HEAD_DIM))
    mask = window_mask(nblk)[None, :, None, None, :, :]
    scores = jnp.where(mask, scores, NEG_INF)
    sink = jnp.broadcast_to(
        sinks.astype(jnp.float32).reshape(N_KV_HEADS, GROUP)[None, None, :, :, None, None],
        scores.shape[:-1] + (1,))
    probs = jax.nn.softmax(jnp.concatenate([scores, sink], axis=-1), axis=-1)[..., :-1]
    out = jnp.einsum("bnhgqk,bnkhd->bnqhgd", probs.astype(v_band.dtype), v_band)
    return x + out.reshape(bsz, s, N_HEADS * HEAD_DIM) @ w_o


def setup_inputs(seed: int = 0) -> dict:
    key = jax.random.key(seed)
    ks = jax.random.split(key, 24)
    f32 = jnp.float32

    def nrm(k, shape, fan_in):
        return jax.random.normal(k, shape, f32) * (fan_in ** -0.5)

    def gain(k, shape):
        return 1.0 + 0.02 * jax.random.normal(k, shape, f32)

    def bias(k, shape):
        return 0.01 * jax.random.normal(k, shape, f32)

    a8 = jax.random.uniform(ks[13], (N_REC, LRU_WIDTH), f32, 0.9, 0.999)
    s_lam = a8 ** (1.0 / LRU_C)
    lru_lambda = jnp.log(s_lam) - jnp.log1p(-s_lam)

    return {
        "x": jax.random.normal(ks[0], (BATCH, SEQ, D_MODEL), f32),
        "norm1_g": gain(ks[1], (DEPTH, D_MODEL)),
        "norm2_g": gain(ks[2], (DEPTH, D_MODEL)),
        "ffn_w_in": nrm(ks[3], (DEPTH, D_MODEL, 2 * D_FF), D_MODEL),
        "ffn_w_out": nrm(ks[4], (DEPTH, D_FF, D_MODEL), D_FF),
        "lru_w_in": nrm(ks[5], (N_REC, D_MODEL, 2 * LRU_WIDTH), D_MODEL),
        "lru_conv_w": nrm(ks[6], (N_REC, CONV_WIDTH, LRU_WIDTH), CONV_WIDTH),
        "lru_conv_b": bias(ks[7], (N_REC, LRU_WIDTH)),
        "lru_w_rg": nrm(ks[8], (N_REC, LRU_BLOCKS, LRU_BLOCK_W, LRU_BLOCK_W), LRU_BLOCK_W),
        "lru_b_rg": bias(ks[9], (N_REC, LRU_WIDTH)),
        "lru_w_ig": nrm(ks[10], (N_REC, LRU_BLOCKS, LRU_BLOCK_W, LRU_BLOCK_W), LRU_BLOCK_W),
        "lru_b_ig": bias(ks[11], (N_REC, LRU_WIDTH)),
        "lru_lambda": lru_lambda,
        "lru_w_out": nrm(ks[12], (N_REC, LRU_WIDTH, D_MODEL), LRU_WIDTH),
        "kv_norm_g": gain(ks[14], (D_MODEL,)),
        "w_kv": nrm(ks[15], (D_MODEL, 2 * N_KV_HEADS * HEAD_DIM), D_MODEL),
        "k_norm_g": gain(ks[16], (HEAD_DIM,)),
        "w_q": nrm(ks[17], (N_ATTN, D_MODEL, N_HEADS * HEAD_DIM), D_MODEL),
        "q_norm_g": gain(ks[18], (N_ATTN, HEAD_DIM)),
        "sinks": 0.5 * jax.random.normal(ks[19], (N_ATTN, N_HEADS), f32),
        "w_o": nrm(ks[20], (N_ATTN, N_HEADS * HEAD_DIM, D_MODEL), N_HEADS * HEAD_DIM),
    }


def reference(x, norm1_g, norm2_g, ffn_w_in, ffn_w_out, lru_w_in, lru_conv_w, lru_conv_b,
              lru_w_rg, lru_b_rg, lru_w_ig, lru_b_ig, lru_lambda, lru_w_out,
              kv_norm_g, w_kv, k_norm_g, w_q, q_norm_g, sinks, w_o):
    k_band = None
    v_band = None
    for layer in range(DEPTH):
        if layer < N_REC:
            i = layer
            x = recurrent_block(x, norm1_g[layer], lru_w_in[i], lru_conv_w[i], lru_conv_b[i],
                                lru_w_rg[i], lru_b_rg[i], lru_w_ig[i], lru_b_ig[i],
                                lru_lambda[i], lru_w_out[i])
        else:
            if layer == N_REC:
                k_band, v_band = shared_kv(x, kv_norm_g, w_kv, k_norm_g)
            j = layer - N_REC
            x = attn_block(x, norm1_g[layer], w_q[j], q_norm_g[j], sinks[j], w_o[j],
                           k_band, v_band)
        x = swiglu_ffn(x, norm2_g[layer], ffn_w_in[layer], ffn_w_out[layer])
    return x
```

```python
import functools
import math

import jax
import jax.numpy as jnp
from jax import lax
from jax.experimental import pallas as pl
from jax.experimental.pallas import tpu as pltpu

NORM_EPS = 1e-6
LRU_C = 8.0
WINDOW = 128
NEG_INF = -1e30
MXU_COLS = 256
VMEM_LIMIT_BYTES = 56 * 1024 * 1024

F32 = jnp.float32
BF16 = jnp.bfloat16


def _params(*sem):
    return pltpu.CompilerParams(dimension_semantics=sem, vmem_limit_bytes=VMEM_LIMIT_BYTES)


def _rms_rows(x, g):
    ms = jnp.mean(x * x, axis=-1, keepdims=True)
    return (x * lax.rsqrt(ms + NORM_EPS)) * g


def _gelu_tanh(x):
    c = math.sqrt(2.0 / math.pi)
    return 0.5 * x * (1.0 + jnp.tanh(c * (x + 0.044715 * (x * x * x))))


def _group_rms(q, gsum_ref, group):
    cols = q.shape[-1]
    outs = []
    for c in range(cols // MXU_COLS):
        qc = q[:, c * MXU_COLS:(c + 1) * MXU_COLS]
        sq = qc * qc
        hi = sq.astype(BF16)
        lo = (sq - hi.astype(F32)).astype(BF16)
        ssq = (jnp.dot(hi, gsum_ref[...], preferred_element_type=F32)
               + jnp.dot(lo, gsum_ref[...], preferred_element_type=F32))
        outs.append(qc * lax.rsqrt(ssq * (1.0 / group) + NORM_EPS))
    return outs[0] if len(outs) == 1 else jnp.concatenate(outs, axis=-1)


def _rec_in_kernel(x_ref, g_ref, wy_ref, wx_ref, oy_ref, ox_ref, h_ref):
    @pl.when(pl.program_id(1) == 0)
    def _():
        h_ref[...] = _rms_rows(x_ref[...], g_ref[...]).astype(BF16)

    h = h_ref[...]
    y = jnp.dot(h, wy_ref[...], preferred_element_type=F32)
    oy_ref[...] = _gelu_tanh(y).astype(BF16)
    ox_ref[...] = jnp.dot(h, wx_ref[...], preferred_element_type=F32)


def _rec_in_proj(x, g, w_in, *, tm, tn):
    t, d = x.shape
    w = w_in.shape[1] // 2
    nj = w // tn
    return pl.pallas_call(
        _rec_in_kernel,
        grid=(t // tm, nj),
        in_specs=[
            pl.BlockSpec((tm, d), lambda i, j: (i, 0)),
            pl.BlockSpec((1, d), lambda i, j: (0, 0)),
            pl.BlockSpec((d, tn), lambda i, j: (0, j)),
            pl.BlockSpec((d, tn), lambda i, j: (0, j + nj)),
        ],
        out_specs=[
            pl.BlockSpec((tm, tn), lambda i, j: (i, j)),
            pl.BlockSpec((tm, tn), lambda i, j: (i, j)),
        ],
        out_shape=[jax.ShapeDtypeStruct((t, w), BF16), jax.ShapeDtypeStruct((t, w), F32)],
        scratch_shapes=[pltpu.VMEM((tm, d), BF16)],
        compiler_params=_params("parallel", "arbitrary"),
        name="rec_in_proj",
    )(x, g, w_in, w_in)


def _rglru_kernel(xbr_ref, gy_ref, cw_ref, cb_ref, wrg_ref, brg_ref, wig_ref, big_ref, lam_ref,
                  o_ref, tail_ref, carry_ref, a_ref, u_ref, *, lane_chunk):
    tc, d = xbr_ref.shape
    taps = cw_ref.shape[0]

    @pl.when(pl.program_id(1) == 0)
    def _():
        tail_ref[...] = jnp.zeros_like(tail_ref)
        carry_ref[...] = jnp.zeros_like(carry_ref)

    xin = xbr_ref[...]
    tail = tail_ref[...]
    row8 = lax.broadcasted_iota(jnp.int32, (8, d), 0)
    xb = xin * cw_ref[taps - 1:taps, :] + cb_ref[...]
    for k in range(1, taps):
        rolled = pltpu.roll(xin, k, axis=0)
        head = jnp.where(row8 < k, pltpu.roll(tail, k, axis=0), rolled[:8])
        xk = jnp.concatenate([head, rolled[8:]], axis=0)
        xb = xb + xk * cw_ref[taps - 1 - k:taps - k, :]
    tail_ref[...] = xin[tc - 8:, :]

    xb16 = xb.astype(BF16)
    nblk = wrg_ref.shape[0]
    bw = d // nblk
    for n in range(nblk):
        sl = slice(n * bw, (n + 1) * bw)
        xs = xb16[:, sl]
        r = jax.nn.sigmoid(jnp.dot(xs, wrg_ref[n], preferred_element_type=F32) + brg_ref[:, sl])
        i = jax.nn.sigmoid(jnp.dot(xs, wig_ref[n], preferred_element_type=F32) + big_ref[:, sl])
        nlam = -lam_ref[:, sl]
        softplus = jnp.maximum(nlam, 0.0) + jnp.log1p(jnp.exp(-jnp.abs(nlam)))
        log_a = (-LRU_C * r) * softplus
        a = jnp.exp(log_a)
        one_minus_a2 = -jnp.tanh(log_a) * (1.0 + a * a)
        a_ref[:, sl] = a
        u_ref[:, sl] = jnp.sqrt(one_minus_a2) * (i * xb[:, sl])

    rowc = lax.broadcasted_iota(jnp.int32, (8, lane_chunk), 0)
    for c in range(d // lane_chunk):
        ls = slice(c * lane_chunk, (c + 1) * lane_chunk)

        def body(g, carry, ls=ls):
            r0 = pl.multiple_of(g * 16, 16)
            hs = []
            for half in range(2):
                av = a_ref[pl.ds(r0 + 8 * half, 8), ls]
                uv = u_ref[pl.ds(r0 + 8 * half, 8), ls]
                for s in (1, 2, 4):
                    keep = rowc >= s
                    a_prev = jnp.where(keep, pltpu.roll(av, s, axis=0), 1.0)
                    u_prev = jnp.where(keep, pltpu.roll(uv, s, axis=0), 0.0)
                    uv = av * u_prev + uv
                    av = av * a_prev
                h = av * carry + uv
                carry = jnp.broadcast_to(h[7:8, :], h.shape)
                hs.append(h)
            h16 = jnp.concatenate(hs, axis=0)
            gate = gy_ref[pl.ds(r0, 16), ls].astype(F32)
            o_ref[pl.ds(r0, 16), ls] = (h16 * gate).astype(BF16)
            return carry

        carry_ref[:, ls] = lax.fori_loop(0, tc // 16, body, carry_ref[:, ls])


def _rglru(xbr, gy, conv_w, conv_b, w_rg, b_rg, w_ig, b_ig, lam, *, batch, tc, lane_chunk):
    t, d = xbr.shape
    s = t // batch
    nt = s // tc
    nblk, bw, _ = w_rg.shape
    row = lambda b, i: (b * nt + i, 0)
    fixed2 = lambda b, i: (0, 0)
    fixed3 = lambda b, i: (0, 0, 0)
    return pl.pallas_call(
        functools.partial(_rglru_kernel, lane_chunk=lane_chunk),
        grid=(batch, nt),
        in_specs=[
            pl.BlockSpec((tc, d), row),
            pl.BlockSpec((tc, d), row),
            pl.BlockSpec(conv_w.shape, fixed2),
            pl.BlockSpec((1, d), fixed2),
            pl.BlockSpec((nblk, bw, bw), fixed3),
            pl.BlockSpec((1, d), fixed2),
            pl.BlockSpec((nblk, bw, bw), fixed3),
            pl.BlockSpec((1, d), fixed2),
            pl.BlockSpec((1, d), fixed2),
        ],
        out_specs=pl.BlockSpec((tc, d), row),
        out_shape=jax.ShapeDtypeStruct((t, d), BF16),
        scratch_shapes=[
            pltpu.VMEM((8, d), F32),
            pltpu.VMEM((8, d), F32),
            pltpu.VMEM((tc, d), F32),
            pltpu.VMEM((tc, d), F32),
        ],
        compiler_params=_params("parallel", "arbitrary"),
        name="rglru_scan",
    )(xbr, gy, conv_w, conv_b, w_rg, b_rg, w_ig, b_ig, lam)


def _proj_res_kernel(m_ref, w_ref, x_ref, o_ref):
    o_ref[...] = x_ref[...] + jnp.dot(m_ref[...], w_ref[...], preferred_element_type=F32)


def _proj_residual(m, w, x, *, tm, tn):
    t, k = m.shape
    n = w.shape[1]
    return pl.pallas_call(
        _proj_res_kernel,
        grid=(t // tm, n // tn),
        in_specs=[
            pl.BlockSpec((tm, k), lambda i, j: (i, 0)),
            pl.BlockSpec((k, tn), lambda i, j: (0, j)),
            pl.BlockSpec((tm, tn), lambda i, j: (i, j)),
        ],
        out_specs=pl.BlockSpec((tm, tn), lambda i, j: (i, j)),
        out_shape=jax.ShapeDtypeStruct((t, n), F32),
        compiler_params=_params("parallel", "arbitrary"),
        name="proj_residual",
    )(m, w, x)


def _ffn_kernel(x_ref, g_ref, wg_ref, wu_ref, wo_ref, o_ref, h_ref):
    @pl.when(pl.program_id(1) == 0)
    def _():
        x = x_ref[...]
        h_ref[...] = _rms_rows(x, g_ref[...]).astype(BF16)
        o_ref[...] = x

    h = h_ref[...]
    gate = jnp.dot(h, wg_ref[...], preferred_element_type=F32)
    up = jnp.dot(h, wu_ref[...], preferred_element_type=F32)
    act = ((gate * jax.nn.sigmoid(gate)) * up).astype(BF16)
    o_ref[...] += jnp.dot(act, wo_ref[...], preferred_element_type=F32)


def _ffn(x, g, w_in, w_out, *, tm, tf):
    t, d = x.shape
    f = w_out.shape[0]
    nf = f // tf
    return pl.pallas_call(
        _ffn_kernel,
        grid=(t // tm, nf),
        in_specs=[
            pl.BlockSpec((tm, d), lambda i, j: (i, 0)),
            pl.BlockSpec((1, d), lambda i, j: (0, 0)),
            pl.BlockSpec((d, tf), lambda i, j: (0, j)),
            pl.BlockSpec((d, tf), lambda i, j: (0, j + nf)),
            pl.BlockSpec((tf, d), lambda i, j: (j, 0)),
        ],
        out_specs=pl.BlockSpec((tm, d), lambda i, j: (i, 0)),
        out_shape=jax.ShapeDtypeStruct((t, d), F32),
        scratch_shapes=[pltpu.VMEM((tm, d), BF16)],
        compiler_params=_params("parallel", "arbitrary"),
        name="swiglu_ffn",
    )(x, g, w_in, w_in, w_out)


def _kv_kernel(x_ref, g_ref, wk_ref, wv_ref, kg_ref, gsum_ref, k_ref, v_ref, *, head_dim):
    h = _rms_rows(x_ref[...], g_ref[...]).astype(BF16)
    k = jnp.dot(h, wk_ref[...], preferred_element_type=F32)
    k_ref[...] = (_group_rms(k, gsum_ref, head_dim) * kg_ref[...]).astype(BF16)
    v_ref[...] = jnp.dot(h, wv_ref[...], preferred_element_type=F32).astype(BF16)


def _kv_proj(x, g, wk2, wv2, kg2, gsum, *, head_dim, tm):
    t, d = x.shape
    n = wk2.shape[1]
    fixed = lambda i: (0, 0)
    return pl.pallas_call(
        functools.partial(_kv_kernel, head_dim=head_dim),
        grid=(t // tm,),
        in_specs=[
            pl.BlockSpec((tm, d), lambda i: (i, 0)),
            pl.BlockSpec((1, d), fixed),
            pl.BlockSpec((d, n), fixed),
            pl.BlockSpec((d, n), fixed),
            pl.BlockSpec((1, n), fixed),
            pl.BlockSpec((MXU_COLS, MXU_COLS), fixed),
        ],
        out_specs=[pl.BlockSpec((tm, n), lambda i: (i, 0)), pl.BlockSpec((tm, n), lambda i: (i, 0))],
        out_shape=[jax.ShapeDtypeStruct((t, n), BF16), jax.ShapeDtypeStruct((t, n), BF16)],
        compiler_params=_params("parallel"),
        name="kv_proj",
    )(x, g, wk2, wv2, kg2, gsum)


def _q_kernel(x_ref, g_ref, w_ref, qg_ref, gsum_ref, o_ref, h_ref, *, head_dim):
    @pl.when(pl.program_id(1) == 0)
    def _():
        h_ref[...] = _rms_rows(x_ref[...], g_ref[...]).astype(BF16)

    q = jnp.dot(h_ref[...], w_ref[...], preferred_element_type=F32)
    qn = _group_rms(q, gsum_ref, head_dim) * qg_ref[...]
    o_ref[...] = (qn * (1.0 / math.sqrt(head_dim))).astype(BF16)


def _q_proj(x, g, w, qg_t, gsum, *, head_dim, tm, tn):
    t, d = x.shape
    n = w.shape[1]
    return pl.pallas_call(
        functools.partial(_q_kernel, head_dim=head_dim),
        grid=(t // tm, n // tn),
        in_specs=[
            pl.BlockSpec((tm, d), lambda i, j: (i, 0)),
            pl.BlockSpec((1, d), lambda i, j: (0, 0)),
            pl.BlockSpec((d, tn), lambda i, j: (0, j)),
            pl.BlockSpec((1, tn), lambda i, j: (0, j)),
            pl.BlockSpec((MXU_COLS, MXU_COLS), lambda i, j: (0, 0)),
        ],
        out_specs=pl.BlockSpec((tm, tn), lambda i, j: (i, j)),
        out_shape=jax.ShapeDtypeStruct((t, n), BF16),
        scratch_shapes=[pltpu.VMEM((tm, d), BF16)],
        compiler_params=_params("parallel", "arbitrary"),
        name="q_proj",
    )(x, g, w, qg_t, gsum)


def _attn_kernel(sink_ref, q_ref, kp_ref, kc_ref, vp_ref, vc_ref, o_ref, *, n_kv, group, head_dim):
    blk = q_ref.shape[0]
    has_prev = pl.program_id(1) > 0
    pair_w = 2 * head_dim
    lane = lax.broadcasted_iota(jnp.int32, (blk, pair_w), 1)
    row = lax.broadcasted_iota(jnp.int32, (blk, blk), 0)
    col = lax.broadcasted_iota(jnp.int32, (blk, blk), 1)
    from_cur = col <= row
    zero = jnp.zeros((), BF16)
    for h in range(n_kv):
        hs = slice(h * pair_w, (h + 1) * pair_w)
        lo_half = lane < head_dim

        def halves(prev, cur):
            return jnp.concatenate([
                jnp.where(lo_half, prev, zero), jnp.where(lo_half, cur, zero),
                jnp.where(lo_half, zero, prev), jnp.where(lo_half, zero, cur)], axis=0)

        kcat = halves(kp_ref[:, hs], kc_ref[:, hs])
        vcat = halves(vp_ref[:, hs], vc_ref[:, hs])
        for p in range(group // 2):
            ps = slice((h * (group // 2) + p) * pair_w, (h * (group // 2) + p + 1) * pair_w)
            s = lax.dot_general(q_ref[:, ps], kcat, (((1,), (1,)), ((), ())),
                                preferred_element_type=F32)
            probs = []
            for e in range(2):
                s_prev = jnp.where(has_prev, s[:, (2 * e) * blk:(2 * e + 1) * blk], NEG_INF)
                s_cur = s[:, (2 * e + 1) * blk:(2 * e + 2) * blk]
                sc = jnp.where(from_cur, s_cur, s_prev)
                sink = sink_ref[h * group + 2 * p + e]
                m = jnp.maximum(jnp.max(sc, axis=-1, keepdims=True), sink)
                pexp = jnp.exp(sc - m)
                denom = jnp.sum(pexp, axis=-1, keepdims=True) + jnp.exp(sink - m)
                pn = pexp * (1.0 / denom)
                probs.append(jnp.where(from_cur, 0.0, pn).astype(BF16))
                probs.append(jnp.where(from_cur, pn, 0.0).astype(BF16))
            pcat = jnp.concatenate(probs, axis=-1)
            o_ref[:, ps] = jnp.dot(pcat, vcat, preferred_element_type=F32).astype(BF16)


def _attention(sinks, q, k2, v2, *, batch, n_kv, group, head_dim):
    t, dq = q.shape
    nb = t // batch // WINDOW
    kw = k2.shape[1]
    cur = lambda b, n: (b * nb + n, 0)
    prev = lambda b, n: (b * nb + jnp.maximum(n - 1, 0), 0)
    return pl.pallas_call(
        functools.partial(_attn_kernel, n_kv=n_kv, group=group, head_dim=head_dim),
        grid=(batch, nb),
        in_specs=[
            pl.BlockSpec(memory_space=pltpu.SMEM),
            pl.BlockSpec((WINDOW, dq), cur),
            pl.BlockSpec((WINDOW, kw), prev),
            pl.BlockSpec((WINDOW, kw), cur),
            pl.BlockSpec((WINDOW, kw), prev),
            pl.BlockSpec((WINDOW, kw), cur),
        ],
        out_specs=pl.BlockSpec((WINDOW, dq), cur),
        out_shape=jax.ShapeDtypeStruct((t, dq), BF16),
        compiler_params=_params("parallel", "arbitrary"),
        name="swa_attention",
    )(sinks, q, k2, k2, v2, v2)


def _tile(n, target):
    if n <= target:
        return n
    best = None
    for c in range(128, target + 1, 128):
        if n % c == 0:
            best = c
    assert best is not None, (n, target)
    return best


def _dup_heads(w, n_kv, head_dim):
    d = w.shape[0]
    w3 = w.reshape(d, n_kv, head_dim)
    return jnp.concatenate([w3, w3], axis=-1).reshape(d, n_kv * 2 * head_dim)


def kernel(x, norm1_g, norm2_g, ffn_w_in, ffn_w_out, lru_w_in, lru_conv_w, lru_conv_b, lru_w_rg,
           lru_b_rg, lru_w_ig, lru_b_ig, lru_lambda, lru_w_out, kv_norm_g, w_kv, k_norm_g, w_q,
           q_norm_g, sinks, w_o):
    batch, seq, d = x.shape
    t = batch * seq
    depth = norm1_g.shape[0]
    n_rec = lru_w_in.shape[0]
    head_dim = k_norm_g.shape[0]
    n_kv = w_kv.shape[1] // (2 * head_dim)
    n_heads = w_q.shape[2] // head_dim
    group = n_heads // n_kv
    assert seq % WINDOW == 0 and group % 2 == 0 and 2 * head_dim == 128
    assert MXU_COLS % head_dim == 0 and (n_kv * 2 * head_dim) % MXU_COLS == 0

    tm = _tile(t, 1024)
    tn = _tile(d, 512)
    tf = _tile(ffn_w_out.shape[1], 512)
    tc = _tile(seq, 256)

    row = lambda v: v.reshape(1, -1).astype(F32)
    gidx = jnp.arange(MXU_COLS) // head_dim
    gsum = (gidx[:, None] == gidx[None, :]).astype(BF16)

    xs = x.reshape(t, d)
    k2 = v2 = None
    for layer in range(depth):
        if layer < n_rec:
            i = layer
            gy, xbr = _rec_in_proj(xs, row(norm1_g[layer]), lru_w_in[i].astype(BF16), tm=tm, tn=tn)
            m = _rglru(xbr, gy, lru_conv_w[i], row(lru_conv_b[i]), lru_w_rg[i].astype(BF16),
                       row(lru_b_rg[i]), lru_w_ig[i].astype(BF16), row(lru_b_ig[i]),
                       row(lru_lambda[i]), batch=batch, tc=tc, lane_chunk=_tile(d, 512))
            xs = _proj_residual(m, lru_w_out[i].astype(BF16), xs, tm=tm, tn=_tile(d, 1024))
        else:
            if layer == n_rec:
                wk, wv = jnp.split(w_kv, 2, axis=-1)
                k2, v2 = _kv_proj(
                    xs, row(kv_norm_g), _dup_heads(wk, n_kv, head_dim).astype(BF16),
                    _dup_heads(wv, n_kv, head_dim).astype(BF16),
                    row(jnp.tile(k_norm_g, 2 * n_kv)), gsum, head_dim=head_dim, tm=_tile(t, 512))
            j = layer - n_rec
            q = _q_proj(xs, row(norm1_g[layer]), w_q[j].astype(BF16),
                        row(jnp.tile(q_norm_g[j], n_heads)), gsum, head_dim=head_dim, tm=tm, tn=tn)
            o = _attention(sinks[j].astype(F32), q, k2, v2, batch=batch, n_kv=n_kv, group=group,
                           head_dim=head_dim)
            xs = _proj_residual(o, w_o[j].astype(BF16), xs, tm=tm, tn=_tile(d, 1024))
        xs = _ffn(xs, row(norm2_g[layer]), ffn_w_in[layer].astype(BF16),
                  ffn_w_out[layer].astype(BF16), tm=tm, tf=tf)
    return xs.reshape(batch, seq, d)
```

```python
import functools
import math

import jax
import jax.numpy as jnp
from jax import lax
from jax.experimental import pallas as pl
from jax.experimental.pallas import tpu as pltpu

NORM_EPS = 1e-6
LRU_C = 8.0
WINDOW = 128
NEG_INF = -1e30
TINY_F32 = 1e-37
MXU_COLS = 256
VMEM_LIMIT_BYTES = 56 * 1024 * 1024

F32 = jnp.float32
BF16 = jnp.bfloat16


def _params(*sem):
    return pltpu.CompilerParams(dimension_semantics=sem, vmem_limit_bytes=VMEM_LIMIT_BYTES)


def _rms_rows(x, g):
    ms = jnp.mean(x * x, axis=-1, keepdims=True)
    return (x * lax.rsqrt(ms + NORM_EPS)) * g


def _gelu_tanh(x):
    c = math.sqrt(2.0 / math.pi)
    return 0.5 * x * (1.0 + jnp.tanh(c * (x + 0.044715 * (x * x * x))))


def _group_rms(q, gsum_ref, group):
    cols = q.shape[-1]
    outs = []
    for c in range(cols // MXU_COLS):
        qc = q[:, c * MXU_COLS:(c + 1) * MXU_COLS]
        sq = qc * qc
        hi = sq.astype(BF16)
        lo = (sq - hi.astype(F32)).astype(BF16)
        ssq = (jnp.dot(hi, gsum_ref[...], preferred_element_type=F32)
               + jnp.dot(lo, gsum_ref[...], preferred_element_type=F32))
        outs.append(qc * lax.rsqrt(ssq * (1.0 / group) + NORM_EPS))
    return outs[0] if len(outs) == 1 else jnp.concatenate(outs, axis=-1)


def _rec_in_kernel(x_ref, g_ref, wy_ref, wx_ref, oy_ref, ox_ref, h_ref):
    @pl.when(pl.program_id(1) == 0)
    def _():
        h_ref[...] = _rms_rows(x_ref[...], g_ref[...]).astype(BF16)

    h = h_ref[...]
    y = jnp.dot(h, wy_ref[...], preferred_element_type=F32)
    oy_ref[...] = _gelu_tanh(y).astype(BF16)
    ox_ref[...] = jnp.dot(h, wx_ref[...], preferred_element_type=F32)


def _rec_in_proj(x, g, w_in, layer, *, tm, tn):
    t, d = x.shape
    w = w_in.shape[2] // 2
    nj = w // tn
    return pl.pallas_call(
        _rec_in_kernel,
        grid=(t // tm, nj),
        in_specs=[
            pl.BlockSpec((tm, d), lambda i, j: (i, 0)),
            pl.BlockSpec((1, d), lambda i, j: (0, 0)),
            pl.BlockSpec((None, d, tn), lambda i, j: (layer, 0, j)),
            pl.BlockSpec((None, d, tn), lambda i, j: (layer, 0, j + nj)),
        ],
        out_specs=[
            pl.BlockSpec((tm, tn), lambda i, j: (i, j)),
            pl.BlockSpec((tm, tn), lambda i, j: (i, j)),
        ],
        out_shape=[jax.ShapeDtypeStruct((t, w), BF16), jax.ShapeDtypeStruct((t, w), F32)],
        scratch_shapes=[pltpu.VMEM((tm, d), BF16)],
        compiler_params=_params("parallel", "arbitrary"),
        name="rec_in_proj",
    )(x, g, w_in, w_in)


def _rglru_kernel(xbr_ref, gy_ref, cw_ref, cb_ref, wrg_ref, brg_ref, wig_ref, big_ref, lam_ref,
                  o_ref, tail_ref, carry_ref, a_ref, u_ref, *, lane_chunk):
    tc, d = xbr_ref.shape
    taps = cw_ref.shape[0]

    @pl.when(pl.program_id(1) == 0)
    def _():
        tail_ref[...] = jnp.zeros_like(tail_ref)
        carry_ref[...] = jnp.zeros_like(carry_ref)

    xin = xbr_ref[...]
    tail = tail_ref[...]
    row8 = lax.broadcasted_iota(jnp.int32, (8, d), 0)
    xb = xin * cw_ref[taps - 1:taps, :] + cb_ref[...]
    for k in range(1, taps):
        rolled = pltpu.roll(xin, k, axis=0)
        head = jnp.where(row8 < k, pltpu.roll(tail, k, axis=0), rolled[:8])
        xk = jnp.concatenate([head, rolled[8:]], axis=0)
        xb = xb + xk * cw_ref[taps - 1 - k:taps - k, :]
    tail_ref[...] = xin[tc - 8:, :]

    xb16 = xb.astype(BF16)
    nblk = wrg_ref.shape[0]
    bw = d // nblk
    for n in range(nblk):
        sl = slice(n * bw, (n + 1) * bw)
        xs = xb16[:, sl]
        r = jax.nn.sigmoid(jnp.dot(xs, wrg_ref[n], preferred_element_type=F32) + brg_ref[:, sl])
        i = jax.nn.sigmoid(jnp.dot(xs, wig_ref[n], preferred_element_type=F32) + big_ref[:, sl])
        nlam = -lam_ref[:, sl]
        softplus = jnp.maximum(nlam, 0.0) + jnp.log1p(jnp.exp(-jnp.abs(nlam)))
        z = r * (LRU_C * softplus)
        a = jnp.exp(-z)
        one_minus_a2 = jnp.tanh(z) * (1.0 + a * a)
        root = one_minus_a2 * lax.rsqrt(jnp.maximum(one_minus_a2, TINY_F32))
        a_ref[:, sl] = a
        u_ref[:, sl] = root * (i * xb[:, sl])

    rowc = lax.broadcasted_iota(jnp.int32, (8, lane_chunk), 0)
    for c in range(d // lane_chunk):
        ls = slice(c * lane_chunk, (c + 1) * lane_chunk)

        def body(g, carry, ls=ls):
            r0 = pl.multiple_of(g * 16, 16)
            hs = []
            for half in range(2):
                av = a_ref[pl.ds(r0 + 8 * half, 8), ls]
                uv = u_ref[pl.ds(r0 + 8 * half, 8), ls]
                for s in (1, 2, 4):
                    keep = rowc >= s
                    a_prev = jnp.where(keep, pltpu.roll(av, s, axis=0), 1.0)
                    u_prev = jnp.where(keep, pltpu.roll(uv, s, axis=0), 0.0)
                    uv = av * u_prev + uv
                    av = av * a_prev
                h = av * carry + uv
                carry = jnp.broadcast_to(h[7:8, :], h.shape)
                hs.append(h)
            h16 = jnp.concatenate(hs, axis=0)
            gate = gy_ref[pl.ds(r0, 16), ls].astype(F32)
            o_ref[pl.ds(r0, 16), ls] = (h16 * gate).astype(BF16)
            return carry

        carry_ref[:, ls] = lax.fori_loop(0, tc // 16, body, carry_ref[:, ls])


def _rglru(xbr, gy, conv_w, conv_b, w_rg, b_rg, w_ig, b_ig, lam, layer, *, batch, tc, lane_chunk):
    t, d = xbr.shape
    s = t // batch
    nt = s // tc
    _, nblk, bw, _ = w_rg.shape
    row = lambda b, i: (b * nt + i, 0)
    fixed2 = lambda b, i: (0, 0)
    fixed3 = lambda b, i: (layer, 0, 0, 0)
    return pl.pallas_call(
        functools.partial(_rglru_kernel, lane_chunk=lane_chunk),
        grid=(batch, nt),
        in_specs=[
            pl.BlockSpec((tc, d), row),
            pl.BlockSpec((tc, d), row),
            pl.BlockSpec(conv_w.shape, fixed2),
            pl.BlockSpec((1, d), fixed2),
            pl.BlockSpec((None, nblk, bw, bw), fixed3),
            pl.BlockSpec((1, d), fixed2),
            pl.BlockSpec((None, nblk, bw, bw), fixed3),
            pl.BlockSpec((1, d), fixed2),
            pl.BlockSpec((1, d), fixed2),
        ],
        out_specs=pl.BlockSpec((tc, d), row),
        out_shape=jax.ShapeDtypeStruct((t, d), BF16),
        scratch_shapes=[
            pltpu.VMEM((8, d), F32),
            pltpu.VMEM((8, d), F32),
            pltpu.VMEM((tc, d), F32),
            pltpu.VMEM((tc, d), F32),
        ],
        compiler_params=_params("parallel", "arbitrary"),
        name="rglru_scan",
    )(xbr, gy, conv_w, conv_b, w_rg, b_rg, w_ig, b_ig, lam)


def _proj_res_kernel(m_ref, w_ref, x_ref, o_ref):
    o_ref[...] = x_ref[...] + jnp.dot(m_ref[...], w_ref[...], preferred_element_type=F32)


def _proj_residual(m, w, layer, x, *, tm, tn):
    t, k = m.shape
    n = w.shape[2]
    return pl.pallas_call(
        _proj_res_kernel,
        grid=(t // tm, n // tn),
        in_specs=[
            pl.BlockSpec((tm, k), lambda i, j: (i, 0)),
            pl.BlockSpec((None, k, tn), lambda i, j: (layer, 0, j)),
            pl.BlockSpec((tm, tn), lambda i, j: (i, j)),
        ],
        out_specs=pl.BlockSpec((tm, tn), lambda i, j: (i, j)),
        out_shape=jax.ShapeDtypeStruct((t, n), F32),
        compiler_params=_params("parallel", "arbitrary"),
        name="proj_residual",
    )(m, w, x)


def _ffn_kernel(x_ref, g_ref, wg_ref, wu_ref, wo_ref, o_ref, h_ref):
    @pl.when(pl.program_id(1) == 0)
    def _():
        x = x_ref[...]
        h_ref[...] = _rms_rows(x, g_ref[...]).astype(BF16)
        o_ref[...] = x

    h = h_ref[...]
    gate = jnp.dot(h, wg_ref[...], preferred_element_type=F32)
    up = jnp.dot(h, wu_ref[...], preferred_element_type=F32)
    act = ((gate * jax.nn.sigmoid(gate)) * up).astype(BF16)
    o_ref[...] += jnp.dot(act, wo_ref[...], preferred_element_type=F32)


def _ffn(x, g, w_in, w_out, layer, *, tm, tf):
    t, d = x.shape
    f = w_out.shape[1]
    nf = f // tf
    return pl.pallas_call(
        _ffn_kernel,
        grid=(t // tm, nf),
        in_specs=[
            pl.BlockSpec((tm, d), lambda i, j: (i, 0)),
            pl.BlockSpec((1, d), lambda i, j: (0, 0)),
            pl.BlockSpec((None, d, tf), lambda i, j: (layer, 0, j)),
            pl.BlockSpec((None, d, tf), lambda i, j: (layer, 0, j + nf)),
            pl.BlockSpec((None, tf, d), lambda i, j: (layer, j, 0)),
        ],
        out_specs=pl.BlockSpec((tm, d), lambda i, j: (i, 0)),
        out_shape=jax.ShapeDtypeStruct((t, d), F32),
        scratch_shapes=[pltpu.VMEM((tm, d), BF16)],
        compiler_params=_params("parallel", "arbitrary"),
        name="swiglu_ffn",
    )(x, g, w_in, w_in, w_out)


def _kv_kernel(x_ref, g_ref, wk_ref, wv_ref, kg_ref, gsum_ref, k_ref, v_ref, *, head_dim):
    h = _rms_rows(x_ref[...], g_ref[...]).astype(BF16)
    k = jnp.dot(h, wk_ref[...], preferred_element_type=F32)
    k_ref[...] = (_group_rms(k, gsum_ref, head_dim) * kg_ref[...]).astype(BF16)
    v_ref[...] = jnp.dot(h, wv_ref[...], preferred_element_type=F32).astype(BF16)


def _kv_proj(x, g, wk2, wv2, kg2, gsum, *, head_dim, tm):
    t, d = x.shape
    n = wk2.shape[1]
    fixed = lambda i: (0, 0)
    return pl.pallas_call(
        functools.partial(_kv_kernel, head_dim=head_dim),
        grid=(t // tm,),
        in_specs=[
            pl.BlockSpec((tm, d), lambda i: (i, 0)),
            pl.BlockSpec((1, d), fixed),
            pl.BlockSpec((d, n), fixed),
            pl.BlockSpec((d, n), fixed),
            pl.BlockSpec((1, n), fixed),
            pl.BlockSpec((MXU_COLS, MXU_COLS), fixed),
        ],
        out_specs=[pl.BlockSpec((tm, n), lambda i: (i, 0)), pl.BlockSpec((tm, n), lambda i: (i, 0))],
        out_shape=[jax.ShapeDtypeStruct((t, n), BF16), jax.ShapeDtypeStruct((t, n), BF16)],
        compiler_params=_params("parallel"),
        name="kv_proj",
    )(x, g, wk2, wv2, kg2, gsum)


def _q_kernel(x_ref, g_ref, w_ref, qg_ref, gsum_ref, o_ref, h_ref, *, head_dim):
    @pl.when(pl.program_id(1) == 0)
    def _():
        h_ref[...] = _rms_rows(x_ref[...], g_ref[...]).astype(BF16)

    q = jnp.dot(h_ref[...], w_ref[...], preferred_element_type=F32)
    qn = _group_rms(q, gsum_ref, head_dim) * qg_ref[...]
    o_ref[...] = (qn * (1.0 / math.sqrt(head_dim))).astype(BF16)


def _q_proj(x, g, w, layer, qg_t, gsum, *, head_dim, tm, tn):
    t, d = x.shape
    n = w.shape[2]
    return pl.pallas_call(
        functools.partial(_q_kernel, head_dim=head_dim),
        grid=(t // tm, n // tn),
        in_specs=[
            pl.BlockSpec((tm, d), lambda i, j: (i, 0)),
            pl.BlockSpec((1, d), lambda i, j: (0, 0)),
            pl.BlockSpec((None, d, tn), lambda i, j: (layer, 0, j)),
            pl.BlockSpec((1, tn), lambda i, j: (0, j)),
            pl.BlockSpec((MXU_COLS, MXU_COLS), lambda i, j: (0, 0)),
        ],
        out_specs=pl.BlockSpec((tm, tn), lambda i, j: (i, j)),
        out_shape=jax.ShapeDtypeStruct((t, n), BF16),
        scratch_shapes=[pltpu.VMEM((tm, d), BF16)],
        compiler_params=_params("parallel", "arbitrary"),
        name="q_proj",
    )(x, g, w, qg_t, gsum)


def _attn_kernel(sink_ref, q_ref, k_ref, v_ref, o_ref, sc_ref, p_ref, es_ref, *,
                 n_kv, group, head_dim, q_blocks):
    blk = WINDOW
    pairs = group // 2
    pair_w = 2 * head_dim
    lane = lax.broadcasted_iota(jnp.int32, (blk, pair_w), 1)
    lo_half = lane < head_dim
    row = lax.broadcasted_iota(jnp.int32, (blk, blk), 0)
    col = lax.broadcasted_iota(jnp.int32, (blk, blk), 1)
    from_cur = col <= row
    cur16 = jnp.where(from_cur, 1.0, 0.0).astype(BF16)
    prev16 = jnp.where(from_cur, 0.0, 1.0).astype(BF16)
    lo_half32 = lane.astype(F32) < float(head_dim)
    zero = jnp.zeros((), BF16)

    def halves(prev, cur):
        return jnp.concatenate([
            jnp.where(lo_half, prev, zero), jnp.where(lo_half, cur, zero),
            jnp.where(lo_half, zero, prev), jnp.where(lo_half, zero, cur)], axis=0)

    lo16 = jnp.where(lo_half32, 1.0, 0.0).astype(BF16)
    hi16 = jnp.where(lo_half32, 0.0, 1.0).astype(BF16)
    ones_cat = jnp.concatenate([lo16, lo16, hi16, hi16], axis=0)

    def block_body(qb, _):
        n = pl.program_id(1) * q_blocks + qb
        r_cur = pl.multiple_of(n * blk, blk)
        r_prev = pl.multiple_of(jnp.maximum(n - 1, 0) * blk, blk)
        r_q = pl.multiple_of(qb * blk, blk)
        prev_bias = jnp.where(n > 0, 0.0, NEG_INF)

        for h in range(n_kv):
            hs = slice(h * pair_w, (h + 1) * pair_w)
            kcat = halves(k_ref[pl.ds(r_prev, blk), hs], k_ref[pl.ds(r_cur, blk), hs])
            qstack = jnp.concatenate(
                [q_ref[pl.ds(r_q, blk), (h * pairs + p) * pair_w:(h * pairs + p + 1) * pair_w]
                 for p in range(pairs)], axis=0)
            s = lax.dot_general(qstack, kcat, (((1,), (1,)), ((), ())),
                                preferred_element_type=F32)
            for p in range(pairs):
                for e in range(2):
                    s_prev = s[p * blk:(p + 1) * blk, (2 * e) * blk:(2 * e + 1) * blk] + prev_bias
                    s_cur = s[p * blk:(p + 1) * blk, (2 * e + 1) * blk:(2 * e + 2) * blk]
                    sc_ref[(h * pairs + p) * 2 + e] = jnp.where(from_cur, s_cur, s_prev)

        for h in range(n_kv):
            for p in range(pairs):
                ms = []
                for e in range(2):
                    sc = sc_ref[(h * pairs + p) * 2 + e]
                    m = jnp.max(sc, axis=-1, keepdims=True)
                    pexp = jnp.exp(sc - m).astype(BF16)
                    p_ref[h, p * blk:(p + 1) * blk, (2 * e) * blk:(2 * e + 1) * blk] = pexp * prev16
                    p_ref[h, p * blk:(p + 1) * blk, (2 * e + 1) * blk:(2 * e + 2) * blk] = pexp * cur16
                    ms.append(sink_ref[h * group + 2 * p + e] - m)
                es_ref[h * pairs + p] = jnp.exp(jnp.where(lo_half32, ms[0], ms[1]))

        for h in range(n_kv):
            hs = slice(h * pair_w, (h + 1) * pair_w)
            vcat = halves(v_ref[pl.ds(r_prev, blk), hs], v_ref[pl.ds(r_cur, blk), hs])
            pv = jnp.dot(p_ref[h], jnp.concatenate([vcat, ones_cat], axis=1),
                         preferred_element_type=F32)
            for p in range(pairs):
                num = pv[p * blk:(p + 1) * blk, :pair_w]
                den = pv[p * blk:(p + 1) * blk, pair_w:] + es_ref[h * pairs + p]
                o_ref[pl.ds(r_q, blk), (h * pairs + p) * pair_w:(h * pairs + p + 1) * pair_w] = (
                    num * (1.0 / den)).astype(BF16)
        return 0

    lax.fori_loop(0, q_blocks, block_body, 0)


def _attention(sinks, q, k2, v2, *, batch, n_kv, group, head_dim, q_blocks):
    t, dq = q.shape
    s = t // batch
    nb = s // WINDOW
    assert nb % q_blocks == 0
    kw = k2.shape[1]
    n_heads = n_kv * group
    steps = nb // q_blocks
    qrow = lambda b, i: (b * steps + i, 0)
    whole = lambda b, i: (b, 0)
    return pl.pallas_call(
        functools.partial(_attn_kernel, n_kv=n_kv, group=group, head_dim=head_dim,
                          q_blocks=q_blocks),
        grid=(batch, steps),
        in_specs=[
            pl.BlockSpec(memory_space=pltpu.SMEM),
            pl.BlockSpec((q_blocks * WINDOW, dq), qrow),
            pl.BlockSpec((s, kw), whole),
            pl.BlockSpec((s, kw), whole),
        ],
        out_specs=pl.BlockSpec((q_blocks * WINDOW, dq), qrow),
        out_shape=jax.ShapeDtypeStruct((t, dq), BF16),
        scratch_shapes=[
            pltpu.VMEM((n_heads, WINDOW, WINDOW), F32),
            pltpu.VMEM((n_kv, (group // 2) * WINDOW, 4 * WINDOW), BF16),
            pltpu.VMEM((n_heads // 2, WINDOW, 2 * head_dim), F32),
        ],
        compiler_params=_params("parallel", "arbitrary"),
        name="swa_attention",
    )(sinks, q, k2, v2)


def _tile(n, target):
    if n <= target:
        return n
    best = None
    for c in range(128, target + 1, 128):
        if n % c == 0:
            best = c
    assert best is not None, (n, target)
    return best


def _dup_heads(w, n_kv, head_dim):
    d = w.shape[0]
    w3 = w.reshape(d, n_kv, head_dim)
    return jnp.concatenate([w3, w3], axis=-1).reshape(d, n_kv * 2 * head_dim)


def kernel(x, norm1_g, norm2_g, ffn_w_in, ffn_w_out, lru_w_in, lru_conv_w, lru_conv_b, lru_w_rg,
           lru_b_rg, lru_w_ig, lru_b_ig, lru_lambda, lru_w_out, kv_norm_g, w_kv, k_norm_g, w_q,
           q_norm_g, sinks, w_o):
    batch, seq, d = x.shape
    t = batch * seq
    depth = norm1_g.shape[0]
    n_rec = lru_w_in.shape[0]
    head_dim = k_norm_g.shape[0]
    n_kv = w_kv.shape[1] // (2 * head_dim)
    n_heads = w_q.shape[2] // head_dim
    group = n_heads // n_kv
    assert seq % WINDOW == 0 and group % 2 == 0 and 2 * head_dim == 128
    assert MXU_COLS % head_dim == 0 and (n_kv * 2 * head_dim) % MXU_COLS == 0

    tm = _tile(t, 1024)
    tn = _tile(d, 512)
    tf = _tile(ffn_w_out.shape[1], 512)
    tc = _tile(seq, 256)

    row = lambda v: v.reshape(1, -1).astype(F32)
    gidx = jnp.arange(MXU_COLS) // head_dim
    gsum = (gidx[:, None] == gidx[None, :]).astype(BF16)

    ffn_w_in16, ffn_w_out16 = ffn_w_in.astype(BF16), ffn_w_out.astype(BF16)
    lru_w_in16, lru_w_out16 = lru_w_in.astype(BF16), lru_w_out.astype(BF16)
    lru_w_rg16, lru_w_ig16 = lru_w_rg.astype(BF16), lru_w_ig.astype(BF16)
    w_q16, w_o16 = w_q.astype(BF16), w_o.astype(BF16)

    xs = x.reshape(t, d)
    k2 = v2 = None
    for layer in range(depth):
        if layer < n_rec:
            i = layer
            gy, xbr = _rec_in_proj(xs, row(norm1_g[layer]), lru_w_in16, i, tm=tm, tn=tn)
            m = _rglru(xbr, gy, lru_conv_w[i], row(lru_conv_b[i]), lru_w_rg16, row(lru_b_rg[i]),
                       lru_w_ig16, row(lru_b_ig[i]), row(lru_lambda[i]), i,
                       batch=batch, tc=tc, lane_chunk=_tile(d, 512))
            xs = _proj_residual(m, lru_w_out16, i, xs, tm=tm, tn=_tile(d, 1024))
        else:
            if layer == n_rec:
                wk, wv = jnp.split(w_kv, 2, axis=-1)
                k2, v2 = _kv_proj(
                    xs, row(kv_norm_g), _dup_heads(wk, n_kv, head_dim).astype(BF16),
                    _dup_heads(wv, n_kv, head_dim).astype(BF16),
                    row(jnp.tile(k_norm_g, 2 * n_kv)), gsum, head_dim=head_dim, tm=_tile(t, 512))
            j = layer - n_rec
            q = _q_proj(xs, row(norm1_g[layer]), w_q16, j, row(jnp.tile(q_norm_g[j], n_heads)),
                        gsum, head_dim=head_dim, tm=tm, tn=tn)
            o = _attention(sinks[j].astype(F32), q, k2, v2, batch=batch, n_kv=n_kv, group=group,
                           head_dim=head_dim, q_blocks=min(4, seq // WINDOW))
            xs = _proj_residual(o, w_o16, j, xs, tm=tm, tn=_tile(d, 1024))
        xs = _ffn(xs, row(norm2_g[layer]), ffn_w_in16, ffn_w_out16, layer, tm=tm, tf=tf)
    return xs.reshape(batch, seq, d)
```

```python
import functools
import math

import jax
import jax.numpy as jnp
from jax import lax
from jax.experimental import pallas as pl
from jax.experimental.pallas import tpu as pltpu

NORM_EPS = 1e-6
LRU_C = 8.0
WINDOW = 128
NEG_INF = -1e30
TINY_F32 = 1e-37
LOG2_E = 1.4426950408889634
MXU_COLS = 256
VMEM_CAP_BYTES = 58 * 1024 * 1024
COMPILER_TEMP_BYTES = 8 * 1024 * 1024

F32 = jnp.float32
BF16 = jnp.bfloat16


def _nbytes(shape, dtype):
    return math.prod(shape) * jnp.dtype(dtype).itemsize


def _params(sem, windows, scratch=()):
    need = sum(2 * _nbytes(*w) for w in windows) + sum(_nbytes(*b) for b in scratch)
    limit = min(need + COMPILER_TEMP_BYTES, VMEM_CAP_BYTES)
    return pltpu.CompilerParams(dimension_semantics=sem, vmem_limit_bytes=limit)


def _rms_rows(x, g):
    ms = jnp.mean(x * x, axis=-1, keepdims=True)
    return (x * lax.rsqrt(ms + NORM_EPS)) * g


def _gelu_tanh(x):
    c = math.sqrt(2.0 / math.pi)
    return 0.5 * x * (1.0 + jnp.tanh(c * (x + 0.044715 * (x * x * x))))


def _group_rms(q, gsum_ref, group):
    cols = q.shape[-1]
    outs = []
    for c in range(cols // MXU_COLS):
        qc = q[:, c * MXU_COLS:(c + 1) * MXU_COLS]
        ssq = jnp.dot((qc * qc).astype(BF16), gsum_ref[...], preferred_element_type=F32)
        outs.append(qc * lax.rsqrt(ssq * (1.0 / group) + NORM_EPS))
    return outs[0] if len(outs) == 1 else jnp.concatenate(outs, axis=-1)


def _rec_in_kernel(x_ref, g_ref, wy_ref, wx_ref, oy_ref, ox_ref, h_ref):
    @pl.when(pl.program_id(1) == 0)
    def _():
        h_ref[...] = _rms_rows(x_ref[...], g_ref[...]).astype(BF16)

    h = h_ref[...]
    y = jnp.dot(h, wy_ref[...], preferred_element_type=F32)
    oy_ref[...] = _gelu_tanh(y).astype(BF16)
    ox_ref[...] = jnp.dot(h, wx_ref[...], preferred_element_type=F32).astype(BF16)


def _rec_in_proj(x, g, w_in, layer, *, tm, tn):
    t, d = x.shape
    w = w_in.shape[2] // 2
    nj = w // tn
    return pl.pallas_call(
        _rec_in_kernel,
        grid=(t // tm, nj),
        in_specs=[
            pl.BlockSpec((tm, d), lambda i, j: (i, 0)),
            pl.BlockSpec((1, d), lambda i, j: (0, 0)),
            pl.BlockSpec((None, d, tn), lambda i, j: (layer, 0, j)),
            pl.BlockSpec((None, d, tn), lambda i, j: (layer, 0, j + nj)),
        ],
        out_specs=[
            pl.BlockSpec((tm, tn), lambda i, j: (i, j)),
            pl.BlockSpec((tm, tn), lambda i, j: (i, j)),
        ],
        out_shape=[jax.ShapeDtypeStruct((t, w), BF16), jax.ShapeDtypeStruct((t, w), BF16)],
        scratch_shapes=[pltpu.VMEM((tm, d), BF16)],
        compiler_params=_params(
            ("parallel", "arbitrary"),
            [((tm, d), F32), ((d, tn), BF16), ((d, tn), BF16), ((tm, tn), BF16), ((tm, tn), BF16)],
            [((tm, d), BF16)]),
        name="rec_in_proj",
    )(x, g, w_in, w_in)


def _rglru_kernel(xbr_ref, gy_ref, perm_ref, permt_ref, cw_ref, cb_ref, wrg_ref, brg_ref, wig_ref,
                  big_ref, lam_ref, o_ref, tail_ref, carry_ref, a_ref, u_ref, gp_ref, m_ref, *,
                  lane_chunk):
    tc, d = xbr_ref.shape
    taps = cw_ref.shape[0]
    seg = tc // 8
    halo = taps - 1

    @pl.when(pl.program_id(1) == 0)
    def _():
        tail_ref[...] = jnp.zeros_like(tail_ref)
        carry_ref[...] = jnp.zeros_like(carry_ref)

    perm = perm_ref[...]
    x0 = jnp.dot(perm, xbr_ref[...], preferred_element_type=F32)
    gp_ref[...] = jnp.dot(perm, gy_ref[...], preferred_element_type=F32)

    first_seg = lax.broadcasted_iota(jnp.int32, (8, d), 0) == 0
    tail = tail_ref[...]
    wrapped = []
    for i in range(halo):
        cur = x0[(seg - halo + i) * 8:(seg - halo + i + 1) * 8]
        prv = tail[i * 8:(i + 1) * 8]
        wrapped.append(jnp.where(first_seg, pltpu.roll(prv, 1, axis=0), pltpu.roll(cur, 1, axis=0)))
    tail_ref[...] = x0[(seg - halo) * 8:]

    xb = x0 * cw_ref[taps - 1:taps, :] + cb_ref[...]
    for k in range(1, taps):
        xk = jnp.concatenate(wrapped[halo - k:] + [x0[:tc - 8 * k]], axis=0)
        xb = xb + xk * cw_ref[taps - 1 - k:taps - k, :]

    xb16 = xb.astype(BF16)
    nblk = wrg_ref.shape[0]
    bw = d // nblk
    for n in range(nblk):
        sl = slice(n * bw, (n + 1) * bw)
        xs = xb16[:, sl]
        r = jax.nn.sigmoid(jnp.dot(xs, wrg_ref[n], preferred_element_type=F32) + brg_ref[:, sl])
        i = jax.nn.sigmoid(jnp.dot(xs, wig_ref[n], preferred_element_type=F32) + big_ref[:, sl])
        nlam = -lam_ref[:, sl]
        softplus = jnp.maximum(nlam, 0.0) + jnp.log1p(jnp.exp(-jnp.abs(nlam)))
        z = r * (LRU_C * softplus)
        a = jnp.exp2(z * (-LOG2_E))
        one_minus_a2 = jnp.tanh(z) * (1.0 + a * a)
        root = one_minus_a2 * lax.rsqrt(jnp.maximum(one_minus_a2, TINY_F32))
        a_ref[:, sl] = a
        u_ref[:, sl] = root * (i * xb[:, sl])

    rowc = lax.broadcasted_iota(jnp.int32, (8, lane_chunk), 0)
    for c in range(d // lane_chunk):
        ls = slice(c * lane_chunk, (c + 1) * lane_chunk)

        h_end = u_ref[0:8, ls]
        e_end = a_ref[0:8, ls]
        for j in range(1, seg):
            av = a_ref[j * 8:(j + 1) * 8, ls]
            h_end = av * h_end + u_ref[j * 8:(j + 1) * 8, ls]
            e_end = av * e_end
            u_ref[j * 8:(j + 1) * 8, ls] = h_end
            a_ref[j * 8:(j + 1) * 8, ls] = e_end

        for s in (1, 2, 4):
            keep = rowc >= s
            e_prev = jnp.where(keep, pltpu.roll(e_end, s, axis=0), 1.0)
            h_prev = jnp.where(keep, pltpu.roll(h_end, s, axis=0), 0.0)
            h_end = e_end * h_prev + h_end
            e_end = e_end * e_prev
        state_in = carry_ref[:, ls]
        after = e_end * state_in + h_end
        seg_in = jnp.where(rowc == 0, state_in, pltpu.roll(after, 1, axis=0))
        carry_ref[:, ls] = jnp.broadcast_to(after[7:8, :], after.shape)
        seg_in2 = jnp.concatenate([seg_in, seg_in], axis=0)

        for jj in range(seg // 2):
            rs = slice(jj * 16, (jj + 1) * 16)
            h = u_ref[rs, ls] + a_ref[rs, ls] * seg_in2
            m_ref[rs, ls] = (h * gp_ref[rs, ls]).astype(BF16)

    o_ref[...] = jnp.dot(permt_ref[...], m_ref[...], preferred_element_type=F32).astype(BF16)


def _rglru(xbr, gy, conv_w, conv_b, w_rg, b_rg, w_ig, b_ig, lam, layer, *, batch, tc, lane_chunk):
    t, d = xbr.shape
    s = t // batch
    nt = s // tc
    _, nblk, bw, _ = w_rg.shape
    taps = conv_w.shape[0]
    assert tc % 16 == 0 and taps - 1 <= tc // 8
    rows = jnp.arange(tc)
    src = (rows % 8) * (tc // 8) + rows // 8
    perm = (src[:, None] == rows[None, :]).astype(BF16)
    row = lambda b, i: (b * nt + i, 0)
    fixed2 = lambda b, i: (0, 0)
    fixed4 = lambda b, i: (layer, 0, 0, 0)
    return pl.pallas_call(
        functools.partial(_rglru_kernel, lane_chunk=lane_chunk),
        grid=(batch, nt),
        in_specs=[
            pl.BlockSpec((tc, d), row),
            pl.BlockSpec((tc, d), row),
            pl.BlockSpec((tc, tc), fixed2),
            pl.BlockSpec((tc, tc), fixed2),
            pl.BlockSpec(conv_w.shape, fixed2),
            pl.BlockSpec((1, d), fixed2),
            pl.BlockSpec((None, nblk, bw, bw), fixed4),
            pl.BlockSpec((1, d), fixed2),
            pl.BlockSpec((None, nblk, bw, bw), fixed4),
            pl.BlockSpec((1, d), fixed2),
            pl.BlockSpec((1, d), fixed2),
        ],
        out_specs=pl.BlockSpec((tc, d), row),
        out_shape=jax.ShapeDtypeStruct((t, d), BF16),
        scratch_shapes=[
            pltpu.VMEM(((taps - 1) * 8, d), F32),
            pltpu.VMEM((8, d), F32),
            pltpu.VMEM((tc, d), F32),
            pltpu.VMEM((tc, d), F32),
            pltpu.VMEM((tc, d), F32),
            pltpu.VMEM((tc, d), BF16),
        ],
        compiler_params=_params(
            ("parallel", "arbitrary"),
            [((tc, d), BF16)] * 3 + [((tc, tc), BF16)] * 2 + [((nblk, bw, bw), BF16)] * 2,
            [((tc, d), F32)] * 3 + [((tc, d), BF16)]),
        name="rglru_scan",
    )(xbr, gy, perm, perm.T, conv_w, conv_b, w_rg, b_rg, w_ig, b_ig, lam)


def _proj_res_kernel(m_ref, w_ref, x_ref, o_ref):
    o_ref[...] = x_ref[...] + jnp.dot(m_ref[...], w_ref[...], preferred_element_type=F32)


def _proj_residual(m, w, layer, x, *, tm, tn):
    t, k = m.shape
    n = w.shape[2]
    return pl.pallas_call(
        _proj_res_kernel,
        grid=(t // tm, n // tn),
        in_specs=[
            pl.BlockSpec((tm, k), lambda i, j: (i, 0)),
            pl.BlockSpec((None, k, tn), lambda i, j: (layer, 0, j)),
            pl.BlockSpec((tm, tn), lambda i, j: (i, j)),
        ],
        out_specs=pl.BlockSpec((tm, tn), lambda i, j: (i, j)),
        out_shape=jax.ShapeDtypeStruct((t, n), F32),
        compiler_params=_params(
            ("parallel", "arbitrary"),
            [((tm, k), BF16), ((k, tn), BF16), ((tm, tn), F32), ((tm, tn), F32)]),
        name="proj_residual",
    )(m, w, x)


def _ffn_kernel(x_ref, g_ref, wg_ref, wu_ref, wo_ref, *rest, cast_next):
    if cast_next:
        nwi_ref, nwo_ref, o_ref, cwi_ref, cwo_ref, h_ref = rest
        cwi_ref[...] = nwi_ref[...].astype(BF16)
        cwo_ref[...] = nwo_ref[...].astype(BF16)
    else:
        o_ref, h_ref = rest

    @pl.when(pl.program_id(1) == 0)
    def _():
        x = x_ref[...]
        h_ref[...] = _rms_rows(x, g_ref[...]).astype(BF16)
        o_ref[...] = x

    h = h_ref[...]
    gate = jnp.dot(h, wg_ref[...], preferred_element_type=F32)
    up = jnp.dot(h, wu_ref[...], preferred_element_type=F32)
    act = ((gate * jax.nn.sigmoid(gate)) * up).astype(BF16)
    o_ref[...] += jnp.dot(act, wo_ref[...], preferred_element_type=F32)


def _ffn_cast_ok(t, d, f, tm, tf):
    ni, nf = t // tm, f // tf
    return (d % ni == 0 and (d // ni) % 128 == 0 and (2 * f) % nf == 0
            and (2 * f // nf) % 128 == 0)


def _ffn(x, g, w_in, w_out, layer, next_w=None, *, tm, tf):
    t, d = x.shape
    f = w_out.shape[1]
    ni, nf = t // tm, f // tf
    in_specs = [
        pl.BlockSpec((tm, d), lambda i, j: (i, 0)),
        pl.BlockSpec((1, d), lambda i, j: (0, 0)),
        pl.BlockSpec((None, d, tf), lambda i, j: (layer, 0, j)),
        pl.BlockSpec((None, d, tf), lambda i, j: (layer, 0, j + nf)),
        pl.BlockSpec((None, tf, d), lambda i, j: (layer, j, 0)),
    ]
    out_specs = [pl.BlockSpec((tm, d), lambda i, j: (i, 0))]
    out_shape = [jax.ShapeDtypeStruct((t, d), F32)]
    args = [x, g, w_in, w_in, w_out]
    windows = [((tm, d), F32)] * 2 + [((d, tf), BF16)] * 3
    if next_w is not None:
        nwi, nwo, nl = next_w
        ri, ci, co = d // ni, 2 * f // nf, d // ni
        in_specs += [pl.BlockSpec((None, ri, ci), lambda i, j: (nl, i, j)),
                     pl.BlockSpec((None, tf, co), lambda i, j: (nl, j, i))]
        out_specs += [pl.BlockSpec((None, ri, ci), lambda i, j: (0, i, j)),
                      pl.BlockSpec((None, tf, co), lambda i, j: (0, j, i))]
        out_shape += [jax.ShapeDtypeStruct((1, d, 2 * f), BF16), jax.ShapeDtypeStruct((1, f, d), BF16)]
        args += [nwi, nwo]
        windows += [((ri, ci), F32), ((tf, co), F32), ((ri, ci), BF16), ((tf, co), BF16)]
    return pl.pallas_call(
        functools.partial(_ffn_kernel, cast_next=next_w is not None),
        grid=(ni, nf),
        in_specs=in_specs,
        out_specs=out_specs,
        out_shape=out_shape,
        scratch_shapes=[pltpu.VMEM((tm, d), BF16)],
        compiler_params=_params(("parallel", "arbitrary"), windows, [((tm, d), BF16)]),
        name="swiglu_ffn",
    )(*args)


def _kv_kernel(x_ref, g_ref, wk_ref, wv_ref, kg_ref, gsum_ref, k_ref, v_ref, *, head_dim):
    h = _rms_rows(x_ref[...], g_ref[...]).astype(BF16)
    k = jnp.dot(h, wk_ref[...], preferred_element_type=F32)
    k_ref[...] = (_group_rms(k, gsum_ref, head_dim) * kg_ref[...]).astype(BF16)
    v_ref[...] = jnp.dot(h, wv_ref[...], preferred_element_type=F32).astype(BF16)


def _kv_proj(x, g, wk2, wv2, kg2, gsum, *, head_dim, tm):
    t, d = x.shape
    n = wk2.shape[1]
    fixed = lambda i: (0, 0)
    return pl.pallas_call(
        functools.partial(_kv_kernel, head_dim=head_dim),
        grid=(t // tm,),
        in_specs=[
            pl.BlockSpec((tm, d), lambda i: (i, 0)),
            pl.BlockSpec((1, d), fixed),
            pl.BlockSpec((d, n), fixed),
            pl.BlockSpec((d, n), fixed),
            pl.BlockSpec((1, n), fixed),
            pl.BlockSpec((MXU_COLS, MXU_COLS), fixed),
        ],
        out_specs=[pl.BlockSpec((tm, n), lambda i: (i, 0)), pl.BlockSpec((tm, n), lambda i: (i, 0))],
        out_shape=[jax.ShapeDtypeStruct((t, n), BF16), jax.ShapeDtypeStruct((t, n), BF16)],
        compiler_params=_params(
            ("parallel",), [((tm, d), F32), ((d, n), BF16), ((d, n), BF16), ((tm, n), BF16), ((tm, n), BF16)]),
        name="kv_proj",
    )(x, g, wk2, wv2, kg2, gsum)


def _q_kernel(x_ref, g_ref, w_ref, qg_ref, gsum_ref, o_ref, h_ref, *, head_dim):
    @pl.when(pl.program_id(1) == 0)
    def _():
        h_ref[...] = _rms_rows(x_ref[...], g_ref[...]).astype(BF16)

    q = jnp.dot(h_ref[...], w_ref[...], preferred_element_type=F32)
    qn = _group_rms(q, gsum_ref, head_dim) * qg_ref[...]
    o_ref[...] = (qn * (1.0 / math.sqrt(head_dim))).astype(BF16)


def _q_proj(x, g, w, layer, qg_t, gsum, *, head_dim, tm, tn):
    t, d = x.shape
    n = w.shape[2]
    return pl.pallas_call(
        functools.partial(_q_kernel, head_dim=head_dim),
        grid=(t // tm, n // tn),
        in_specs=[
            pl.BlockSpec((tm, d), lambda i, j: (i, 0)),
            pl.BlockSpec((1, d), lambda i, j: (0, 0)),
            pl.BlockSpec((None, d, tn), lambda i, j: (layer, 0, j)),
            pl.BlockSpec((1, tn), lambda i, j: (0, j)),
            pl.BlockSpec((MXU_COLS, MXU_COLS), lambda i, j: (0, 0)),
        ],
        out_specs=pl.BlockSpec((tm, tn), lambda i, j: (i, j)),
        out_shape=jax.ShapeDtypeStruct((t, n), BF16),
        scratch_shapes=[pltpu.VMEM((tm, d), BF16)],
        compiler_params=_params(
            ("parallel", "arbitrary"), [((tm, d), F32), ((d, tn), BF16), ((tm, tn), BF16)],
            [((tm, d), BF16)]),
        name="q_proj",
    )(x, g, w, qg_t, gsum)


def _attn_kernel(sink_ref, q_ref, k_ref, v_ref, o_ref, sc_ref, p_ref, es_ref, *,
                 n_kv, group, head_dim, q_blocks):
    blk = WINDOW
    pairs = group // 2
    pair_w = 2 * head_dim
    lane = lax.broadcasted_iota(jnp.int32, (blk, pair_w), 1)
    lo_half = lane < head_dim
    row = lax.broadcasted_iota(jnp.int32, (blk, blk), 0)
    col = lax.broadcasted_iota(jnp.int32, (blk, blk), 1)
    from_cur = col <= row
    cur16 = jnp.where(from_cur, 1.0, 0.0).astype(BF16)
    prev16 = jnp.where(from_cur, 0.0, 1.0).astype(BF16)
    lo_half32 = lane.astype(F32) < float(head_dim)
    zero = jnp.zeros((), BF16)

    def halves(prev, cur):
        return jnp.concatenate([
            jnp.where(lo_half, prev, zero), jnp.where(lo_half, cur, zero),
            jnp.where(lo_half, zero, prev), jnp.where(lo_half, zero, cur)], axis=0)

    lo16 = jnp.where(lo_half32, 1.0, 0.0).astype(BF16)
    hi16 = jnp.where(lo_half32, 0.0, 1.0).astype(BF16)
    ones_cat = jnp.concatenate([lo16, lo16, hi16, hi16], axis=0)

    def block_body(qb, _):
        n = pl.program_id(1) * q_blocks + qb
        r_cur = pl.multiple_of(n * blk, blk)
        r_prev = pl.multiple_of(jnp.maximum(n - 1, 0) * blk, blk)
        r_q = pl.multiple_of(qb * blk, blk)
        prev_bias = jnp.where(n > 0, 0.0, NEG_INF)

        for h in range(n_kv):
            hs = slice(h * pair_w, (h + 1) * pair_w)
            kcat = halves(k_ref[pl.ds(r_prev, blk), hs], k_ref[pl.ds(r_cur, blk), hs])
            qstack = jnp.concatenate(
                [q_ref[pl.ds(r_q, blk), (h * pairs + p) * pair_w:(h * pairs + p + 1) * pair_w]
                 for p in range(pairs)], axis=0)
            s = lax.dot_general(qstack, kcat, (((1,), (1,)), ((), ())),
                                preferred_element_type=F32)
            for p in range(pairs):
                for e in range(2):
                    s_prev = s[p * blk:(p + 1) * blk, (2 * e) * blk:(2 * e + 1) * blk] + prev_bias
                    s_cur = s[p * blk:(p + 1) * blk, (2 * e + 1) * blk:(2 * e + 2) * blk]
                    sc_ref[(h * pairs + p) * 2 + e] = jnp.where(from_cur, s_cur, s_prev)

        for h in range(n_kv):
            for p in range(pairs):
                ms = []
                for e in range(2):
                    sc = sc_ref[(h * pairs + p) * 2 + e]
                    m = jnp.max(sc, axis=-1, keepdims=True)
                    pexp = jnp.exp(sc - m).astype(BF16)
                    p_ref[h, p * blk:(p + 1) * blk, (2 * e) * blk:(2 * e + 1) * blk] = pexp * prev16
                    p_ref[h, p * blk:(p + 1) * blk, (2 * e + 1) * blk:(2 * e + 2) * blk] = pexp * cur16
                    ms.append(sink_ref[h * group + 2 * p + e] - m)
                es_ref[h * pairs + p] = jnp.exp(jnp.where(lo_half32, ms[0], ms[1]))

        for h in range(n_kv):
            hs = slice(h * pair_w, (h + 1) * pair_w)
            vcat = halves(v_ref[pl.ds(r_prev, blk), hs], v_ref[pl.ds(r_cur, blk), hs])
            pv = jnp.dot(p_ref[h], jnp.concatenate([vcat, ones_cat], axis=1),
                         preferred_element_type=F32)
            for p in range(pairs):
                num = pv[p * blk:(p + 1) * blk, :pair_w]
                den = pv[p * blk:(p + 1) * blk, pair_w:] + es_ref[h * pairs + p]
                o_ref[pl.ds(r_q, blk), (h * pairs + p) * pair_w:(h * pairs + p + 1) * pair_w] = (
                    num * (1.0 / den)).astype(BF16)
        return 0

    lax.fori_loop(0, q_blocks, block_body, 0)


def _attention(sinks, q, k2, v2, *, batch, n_kv, group, head_dim, q_blocks):
    t, dq = q.shape
    s = t // batch
    nb = s // WINDOW
    assert nb % q_blocks == 0
    kw = k2.shape[1]
    n_heads = n_kv * group
    steps = nb // q_blocks
    qrow = lambda b, i: (b * steps + i, 0)
    whole = lambda b, i: (b, 0)
    return pl.pallas_call(
        functools.partial(_attn_kernel, n_kv=n_kv, group=group, head_dim=head_dim,
                          q_blocks=q_blocks),
        grid=(batch, steps),
        in_specs=[
            pl.BlockSpec(memory_space=pltpu.SMEM),
            pl.BlockSpec((q_blocks * WINDOW, dq), qrow),
            pl.BlockSpec((s, kw), whole),
            pl.BlockSpec((s, kw), whole),
        ],
        out_specs=pl.BlockSpec((q_blocks * WINDOW, dq), qrow),
        out_shape=jax.ShapeDtypeStruct((t, dq), BF16),
        scratch_shapes=[
            pltpu.VMEM((n_heads, WINDOW, WINDOW), F32),
            pltpu.VMEM((n_kv, (group // 2) * WINDOW, 4 * WINDOW), BF16),
            pltpu.VMEM((n_heads // 2, WINDOW, 2 * head_dim), F32),
        ],
        compiler_params=_params(
            ("parallel", "arbitrary"),
            [((q_blocks * WINDOW, dq), BF16)] * 2 + [((s, kw), BF16)] * 2,
            [((n_heads, WINDOW, WINDOW), F32), ((n_heads, WINDOW, WINDOW), BF16),
             ((n_heads // 2, WINDOW, 2 * head_dim), F32)]),
        name="swa_attention",
    )(sinks, q, k2, v2)


def _tile(n, target):
    if n <= target:
        return n
    best = None
    for c in range(128, target + 1, 128):
        if n % c == 0:
            best = c
    assert best is not None, (n, target)
    return best


def _dup_heads(w, n_kv, head_dim):
    d = w.shape[0]
    w3 = w.reshape(d, n_kv, head_dim)
    return jnp.concatenate([w3, w3], axis=-1).reshape(d, n_kv * 2 * head_dim)


def kernel(x, norm1_g, norm2_g, ffn_w_in, ffn_w_out, lru_w_in, lru_conv_w, lru_conv_b, lru_w_rg,
           lru_b_rg, lru_w_ig, lru_b_ig, lru_lambda, lru_w_out, kv_norm_g, w_kv, k_norm_g, w_q,
           q_norm_g, sinks, w_o):
    batch, seq, d = x.shape
    t = batch * seq
    depth = norm1_g.shape[0]
    n_rec = lru_w_in.shape[0]
    head_dim = k_norm_g.shape[0]
    n_kv = w_kv.shape[1] // (2 * head_dim)
    n_heads = w_q.shape[2] // head_dim
    group = n_heads // n_kv
    assert seq % WINDOW == 0 and group % 2 == 0 and 2 * head_dim == 128
    assert MXU_COLS % head_dim == 0 and (n_kv * 2 * head_dim) % MXU_COLS == 0

    tm = _tile(t, 1024)
    tn = _tile(d, 1024)
    f = ffn_w_out.shape[1]
    tf = _tile(f, 512)
    tc = _tile(seq, 256)

    row = lambda v: v.reshape(1, -1).astype(F32)
    gidx = jnp.arange(MXU_COLS) // head_dim
    gsum = (gidx[:, None] == gidx[None, :]).astype(BF16)

    lru_w_in16, lru_w_out16 = lru_w_in.astype(BF16), lru_w_out.astype(BF16)
    lru_w_rg16, lru_w_ig16 = lru_w_rg.astype(BF16), lru_w_ig.astype(BF16)
    w_q16, w_o16 = w_q.astype(BF16), w_o.astype(BF16)
    chain = _ffn_cast_ok(t, d, f, tm, tf)
    if chain:
        ffn_w16 = (ffn_w_in[:1].astype(BF16), ffn_w_out[:1].astype(BF16))
    else:
        ffn_w16 = (ffn_w_in.astype(BF16), ffn_w_out.astype(BF16))

    xs = x.reshape(t, d)
    k2 = v2 = None
    for layer in range(depth):
        if layer < n_rec:
            i = layer
            gy, xbr = _rec_in_proj(xs, row(norm1_g[layer]), lru_w_in16, i, tm=tm, tn=tn)
            m = _rglru(xbr, gy, lru_conv_w[i], row(lru_conv_b[i]), lru_w_rg16, row(lru_b_rg[i]),
                       lru_w_ig16, row(lru_b_ig[i]), row(lru_lambda[i]), i,
                       batch=batch, tc=tc, lane_chunk=_tile(d, 512))
            xs = _proj_residual(m, lru_w_out16, i, xs, tm=_tile(t, 512), tn=d)
        else:
            if layer == n_rec:
                wk, wv = jnp.split(w_kv, 2, axis=-1)
                k2, v2 = _kv_proj(
                    xs, row(kv_norm_g), _dup_heads(wk, n_kv, head_dim).astype(BF16),
                    _dup_heads(wv, n_kv, head_dim).astype(BF16),
                    row(jnp.tile(k_norm_g, 2 * n_kv)), gsum, head_dim=head_dim, tm=_tile(t, 512))
            j = layer - n_rec
            q = _q_proj(xs, row(norm1_g[layer]), w_q16, j, row(jnp.tile(q_norm_g[j], n_heads)),
                        gsum, head_dim=head_dim, tm=tm, tn=tn)
            o = _attention(sinks[j].astype(F32), q, k2, v2, batch=batch, n_kv=n_kv, group=group,
                           head_dim=head_dim, q_blocks=min(4, seq // WINDOW))
            xs = _proj_residual(o, w_o16, j, xs, tm=_tile(t, 512), tn=d)
        if chain:
            nxt = (ffn_w_in, ffn_w_out, layer + 1) if layer + 1 < depth else None
            outs = _ffn(xs, row(norm2_g[layer]), ffn_w16[0], ffn_w16[1], 0, nxt, tm=tm, tf=tf)
            xs, ffn_w16 = outs[0], tuple(outs[1:])
        else:
            xs = _ffn(xs, row(norm2_g[layer]), ffn_w16[0], ffn_w16[1], layer, tm=tm, tf=tf)[0]
    return xs.reshape(batch, seq, d)
```

```python
import functools
import math

import jax
import jax.numpy as jnp
from jax import lax
from jax.experimental import pallas as pl
from jax.experimental.pallas import tpu as pltpu

NORM_EPS = 1e-6
LRU_C = 8.0
WINDOW = 128
NEG_INF = -1e30
TINY_F32 = 1e-37
LOG2_E = 1.4426950408889634
MXU_COLS = 256
VMEM_CAP_BYTES = 58 * 1024 * 1024
COMPILER_TEMP_BYTES = 8 * 1024 * 1024

F32 = jnp.float32
BF16 = jnp.bfloat16


def _nbytes(shape, dtype):
    return math.prod(shape) * jnp.dtype(dtype).itemsize


def _params(sem, windows, scratch=()):
    need = sum(2 * _nbytes(*w) for w in windows) + sum(_nbytes(*b) for b in scratch)
    limit = min(need + COMPILER_TEMP_BYTES, VMEM_CAP_BYTES)
    return pltpu.CompilerParams(dimension_semantics=sem, vmem_limit_bytes=limit)


def _rms_rows(x, g):
    ms = jnp.mean(x * x, axis=-1, keepdims=True)
    return (x * lax.rsqrt(ms + NORM_EPS)) * g


def _gelu_tanh(x):
    c = math.sqrt(2.0 / math.pi)
    return 0.5 * x * (1.0 + jnp.tanh(c * (x + 0.044715 * (x * x * x))))


def _group_rms(q, gsum_ref, group):
    cols = q.shape[-1]
    outs = []
    for c in range(cols // MXU_COLS):
        qc = q[:, c * MXU_COLS:(c + 1) * MXU_COLS]
        ssq = jnp.dot((qc * qc).astype(BF16), gsum_ref[...], preferred_element_type=F32)
        outs.append(qc * lax.rsqrt(ssq * (1.0 / group) + NORM_EPS))
    return outs[0] if len(outs) == 1 else jnp.concatenate(outs, axis=-1)


def _rec_in_kernel(x_ref, g_ref, wy_ref, wx_ref, oy_ref, ox_ref, h_ref):
    @pl.when(pl.program_id(1) == 0)
    def _():
        h_ref[...] = _rms_rows(x_ref[...], g_ref[...]).astype(BF16)

    h = h_ref[...]
    y = jnp.dot(h, wy_ref[...], preferred_element_type=F32)
    oy_ref[...] = _gelu_tanh(y).astype(BF16)
    ox_ref[...] = jnp.dot(h, wx_ref[...], preferred_element_type=F32).astype(BF16)


def _rec_in_proj(x, g, w_in, layer, *, tm, tn):
    t, d = x.shape
    w = w_in.shape[2] // 2
    nj = w // tn
    return pl.pallas_call(
        _rec_in_kernel,
        grid=(t // tm, nj),
        in_specs=[
            pl.BlockSpec((tm, d), lambda i, j: (i, 0)),
            pl.BlockSpec((1, d), lambda i, j: (0, 0)),
            pl.BlockSpec((None, d, tn), lambda i, j: (layer, 0, j)),
            pl.BlockSpec((None, d, tn), lambda i, j: (layer, 0, j + nj)),
        ],
        out_specs=[
            pl.BlockSpec((tm, tn), lambda i, j: (i, j)),
            pl.BlockSpec((tm, tn), lambda i, j: (i, j)),
        ],
        out_shape=[jax.ShapeDtypeStruct((t, w), BF16), jax.ShapeDtypeStruct((t, w), BF16)],
        scratch_shapes=[pltpu.VMEM((tm, d), BF16)],
        compiler_params=_params(
            ("parallel", "arbitrary"),
            [((tm, d), F32), ((d, tn), BF16), ((d, tn), BF16), ((tm, tn), BF16), ((tm, tn), BF16)],
            [((tm, d), BF16)]),
        name="rec_in_proj",
    )(x, g, w_in, w_in)


def _rglru_kernel(xbr_ref, gy_ref, perm_ref, permt_ref, cw_ref, cb_ref, wrg_ref, brg_ref, wig_ref,
                  big_ref, lam_ref, *rest, lane_chunk, cast_w):
    if cast_w:
        fwi_ref, fwo_ref, o_ref, cwi_ref, cwo_ref, tail_ref, carry_ref, a_ref, u_ref, gp_ref, m_ref = rest
        cwi_ref[...] = fwi_ref[...].astype(BF16)
        cwo_ref[...] = fwo_ref[...].astype(BF16)
    else:
        o_ref, tail_ref, carry_ref, a_ref, u_ref, gp_ref, m_ref = rest
    tc, d = xbr_ref.shape
    taps = cw_ref.shape[0]
    seg = tc // 8
    halo = taps - 1

    @pl.when(pl.program_id(1) == 0)
    def _():
        tail_ref[...] = jnp.zeros_like(tail_ref)
        carry_ref[...] = jnp.zeros_like(carry_ref)

    perm = perm_ref[...]
    x0 = jnp.dot(perm, xbr_ref[...], preferred_element_type=F32)
    gp_ref[...] = jnp.dot(perm, gy_ref[...], preferred_element_type=F32)

    first_seg = lax.broadcasted_iota(jnp.int32, (8, d), 0) == 0
    tail = tail_ref[...]
    wrapped = []
    for i in range(halo):
        cur = x0[(seg - halo + i) * 8:(seg - halo + i + 1) * 8]
        prv = tail[i * 8:(i + 1) * 8]
        wrapped.append(jnp.where(first_seg, pltpu.roll(prv, 1, axis=0), pltpu.roll(cur, 1, axis=0)))
    tail_ref[...] = x0[(seg - halo) * 8:]

    xb = x0 * cw_ref[taps - 1:taps, :] + cb_ref[...]
    for k in range(1, taps):
        xk = jnp.concatenate(wrapped[halo - k:] + [x0[:tc - 8 * k]], axis=0)
        xb = xb + xk * cw_ref[taps - 1 - k:taps - k, :]

    xb16 = xb.astype(BF16)
    nblk = wrg_ref.shape[0]
    bw = d // nblk
    for n in range(nblk):
        sl = slice(n * bw, (n + 1) * bw)
        xs = xb16[:, sl]
        r = jax.nn.sigmoid(jnp.dot(xs, wrg_ref[n], preferred_element_type=F32) + brg_ref[:, sl])
        i = jax.nn.sigmoid(jnp.dot(xs, wig_ref[n], preferred_element_type=F32) + big_ref[:, sl])
        nlam = -lam_ref[:, sl]
        softplus = jnp.maximum(nlam, 0.0) + jnp.log1p(jnp.exp(-jnp.abs(nlam)))
        z = r * (LRU_C * softplus)
        a = jnp.exp2(z * (-LOG2_E))
        one_minus_a2 = jnp.tanh(z) * (1.0 + a * a)
        root = one_minus_a2 * lax.rsqrt(jnp.maximum(one_minus_a2, TINY_F32))
        a_ref[:, sl] = a
        u_ref[:, sl] = root * (i * xb[:, sl])

    rowc = lax.broadcasted_iota(jnp.int32, (8, lane_chunk), 0)
    for c in range(d // lane_chunk):
        ls = slice(c * lane_chunk, (c + 1) * lane_chunk)

        h_end = u_ref[0:8, ls]
        e_end = a_ref[0:8, ls]
        for j in range(1, seg):
            av = a_ref[j * 8:(j + 1) * 8, ls]
            h_end = av * h_end + u_ref[j * 8:(j + 1) * 8, ls]
            e_end = av * e_end
            u_ref[j * 8:(j + 1) * 8, ls] = h_end
            a_ref[j * 8:(j + 1) * 8, ls] = e_end

        for s in (1, 2, 4):
            keep = rowc >= s
            e_prev = jnp.where(keep, pltpu.roll(e_end, s, axis=0), 1.0)
            h_prev = jnp.where(keep, pltpu.roll(h_end, s, axis=0), 0.0)
            h_end = e_end * h_prev + h_end
            e_end = e_end * e_prev
        state_in = carry_ref[:, ls]
        after = e_end * state_in + h_end
        seg_in = jnp.where(rowc == 0, state_in, pltpu.roll(after, 1, axis=0))
        carry_ref[:, ls] = jnp.broadcast_to(after[7:8, :], after.shape)
        seg_in2 = jnp.concatenate([seg_in, seg_in], axis=0)

        for jj in range(seg // 2):
            rs = slice(jj * 16, (jj + 1) * 16)
            h = u_ref[rs, ls] + a_ref[rs, ls] * seg_in2
            m_ref[rs, ls] = (h * gp_ref[rs, ls]).astype(BF16)

    o_ref[...] = jnp.dot(permt_ref[...], m_ref[...], preferred_element_type=F32).astype(BF16)


def _rglru_cast_ok(batch, nt, d, f):
    return (d % nt == 0 and (d // nt) % 128 == 0 and f % batch == 0 and (f // batch) % 128 == 0)


def _rglru(xbr, gy, conv_w, conv_b, w_rg, b_rg, w_ig, b_ig, lam, layer, cast_w=None, *,
           batch, tc, lane_chunk):
    t, d = xbr.shape
    s = t // batch
    nt = s // tc
    _, nblk, bw, _ = w_rg.shape
    taps = conv_w.shape[0]
    assert tc % 16 == 0 and taps - 1 <= tc // 8
    rows = jnp.arange(tc)
    src = (rows % 8) * (tc // 8) + rows // 8
    perm = (src[:, None] == rows[None, :]).astype(BF16)
    row = lambda b, i: (b * nt + i, 0)
    fixed2 = lambda b, i: (0, 0)
    fixed4 = lambda b, i: (layer, 0, 0, 0)
    in_specs = [
        pl.BlockSpec((tc, d), row),
        pl.BlockSpec((tc, d), row),
        pl.BlockSpec((tc, tc), fixed2),
        pl.BlockSpec((tc, tc), fixed2),
        pl.BlockSpec(conv_w.shape, fixed2),
        pl.BlockSpec((1, d), fixed2),
        pl.BlockSpec((None, nblk, bw, bw), fixed4),
        pl.BlockSpec((1, d), fixed2),
        pl.BlockSpec((None, nblk, bw, bw), fixed4),
        pl.BlockSpec((1, d), fixed2),
        pl.BlockSpec((1, d), fixed2),
    ]
    out_specs = [pl.BlockSpec((tc, d), row)]
    out_shape = [jax.ShapeDtypeStruct((t, d), BF16)]
    args = [xbr, gy, perm, perm.T, conv_w, conv_b, w_rg, b_rg, w_ig, b_ig, lam]
    windows = [((tc, d), BF16)] * 3 + [((tc, tc), BF16)] * 2 + [((nblk, bw, bw), BF16)] * 2
    if cast_w is not None:
        fwi, fwo, fl = cast_w
        f = fwo.shape[1]
        ri, ci, ro, co = d // nt, 2 * f // batch, f // batch, d // nt
        in_specs += [pl.BlockSpec((None, ri, ci), lambda b, i: (fl, i, b)),
                     pl.BlockSpec((None, ro, co), lambda b, i: (fl, b, i))]
        out_specs += [pl.BlockSpec((None, ri, ci), lambda b, i: (0, i, b)),
                      pl.BlockSpec((None, ro, co), lambda b, i: (0, b, i))]
        out_shape += [jax.ShapeDtypeStruct((1, d, 2 * f), BF16), jax.ShapeDtypeStruct((1, f, d), BF16)]
        args += [fwi, fwo]
        windows += [((ri, ci), F32), ((ro, co), F32), ((ri, ci), BF16), ((ro, co), BF16)]
    return pl.pallas_call(
        functools.partial(_rglru_kernel, lane_chunk=lane_chunk, cast_w=cast_w is not None),
        grid=(batch, nt),
        in_specs=in_specs,
        out_specs=out_specs,
        out_shape=out_shape,
        scratch_shapes=[
            pltpu.VMEM(((taps - 1) * 8, d), F32),
            pltpu.VMEM((8, d), F32),
            pltpu.VMEM((tc, d), F32),
            pltpu.VMEM((tc, d), F32),
            pltpu.VMEM((tc, d), F32),
            pltpu.VMEM((tc, d), BF16),
        ],
        compiler_params=_params(("parallel", "arbitrary"), windows,
                                [((tc, d), F32)] * 3 + [((tc, d), BF16)]),
        name="rglru_scan",
    )(*args)


def _proj_res_kernel(m_ref, w_ref, x_ref, o_ref):
    o_ref[...] = x_ref[...] + jnp.dot(m_ref[...], w_ref[...], preferred_element_type=F32)


def _proj_residual(m, w, layer, x, *, tm, tn):
    t, k = m.shape
    n = w.shape[2]
    return pl.pallas_call(
        _proj_res_kernel,
        grid=(t // tm, n // tn),
        in_specs=[
            pl.BlockSpec((tm, k), lambda i, j: (i, 0)),
            pl.BlockSpec((None, k, tn), lambda i, j: (layer, 0, j)),
            pl.BlockSpec((tm, tn), lambda i, j: (i, j)),
        ],
        out_specs=pl.BlockSpec((tm, tn), lambda i, j: (i, j)),
        out_shape=jax.ShapeDtypeStruct((t, n), F32),
        compiler_params=_params(
            ("parallel", "arbitrary"),
            [((tm, k), BF16), ((k, tn), BF16), ((tm, tn), F32), ((tm, tn), F32)]),
        name="proj_residual",
    )(m, w, x)


def _ffn_kernel(x_ref, g_ref, wg_ref, wu_ref, wo_ref, *rest, cast_next):
    if cast_next:
        nwi_ref, nwo_ref, o_ref, cwi_ref, cwo_ref, h_ref = rest
        cwi_ref[...] = nwi_ref[...].astype(BF16)
        cwo_ref[...] = nwo_ref[...].astype(BF16)
    else:
        o_ref, h_ref = rest

    @pl.when(pl.program_id(1) == 0)
    def _():
        x = x_ref[...]
        h_ref[...] = _rms_rows(x, g_ref[...]).astype(BF16)
        o_ref[...] = x

    h = h_ref[...]
    gate = jnp.dot(h, wg_ref[...], preferred_element_type=F32)
    up = jnp.dot(h, wu_ref[...], preferred_element_type=F32)
    act = ((gate * jax.nn.sigmoid(gate)) * up).astype(BF16)
    o_ref[...] += jnp.dot(act, wo_ref[...], preferred_element_type=F32)


def _ffn_cast_ok(t, d, f, tm, tf):
    ni, nf = t // tm, f // tf
    return (d % ni == 0 and (d // ni) % 128 == 0 and (2 * f) % nf == 0
            and (2 * f // nf) % 128 == 0)


def _ffn(x, g, w_in, w_out, layer, next_w=None, *, tm, tf):
    t, d = x.shape
    f = w_out.shape[1]
    ni, nf = t // tm, f // tf
    in_specs = [
        pl.BlockSpec((tm, d), lambda i, j: (i, 0)),
        pl.BlockSpec((1, d), lambda i, j: (0, 0)),
        pl.BlockSpec((None, d, tf), lambda i, j: (layer, 0, j)),
        pl.BlockSpec((None, d, tf), lambda i, j: (layer, 0, j + nf)),
        pl.BlockSpec((None, tf, d), lambda i, j: (layer, j, 0)),
    ]
    out_specs = [pl.BlockSpec((tm, d), lambda i, j: (i, 0))]
    out_shape = [jax.ShapeDtypeStruct((t, d), F32)]
    args = [x, g, w_in, w_in, w_out]
    windows = [((tm, d), F32)] * 2 + [((d, tf), BF16)] * 3
    if next_w is not None:
        nwi, nwo, nl = next_w
        ri, ci, co = d // ni, 2 * f // nf, d // ni
        in_specs += [pl.BlockSpec((None, ri, ci), lambda i, j: (nl, i, j)),
                     pl.BlockSpec((None, tf, co), lambda i, j: (nl, j, i))]
        out_specs += [pl.BlockSpec((None, ri, ci), lambda i, j: (0, i, j)),
                      pl.BlockSpec((None, tf, co), lambda i, j: (0, j, i))]
        out_shape += [jax.ShapeDtypeStruct((1, d, 2 * f), BF16), jax.ShapeDtypeStruct((1, f, d), BF16)]
        args += [nwi, nwo]
        windows += [((ri, ci), F32), ((tf, co), F32), ((ri, ci), BF16), ((tf, co), BF16)]
    return pl.pallas_call(
        functools.partial(_ffn_kernel, cast_next=next_w is not None),
        grid=(ni, nf),
        in_specs=in_specs,
        out_specs=out_specs,
        out_shape=out_shape,
        scratch_shapes=[pltpu.VMEM((tm, d), BF16)],
        compiler_params=_params(("parallel", "arbitrary"), windows, [((tm, d), BF16)]),
        name="swiglu_ffn",
    )(*args)


def _kv_kernel(x_ref, g_ref, wk_ref, wv_ref, kg_ref, gsum_ref, k_ref, v_ref, *, head_dim):
    h = _rms_rows(x_ref[...], g_ref[...]).astype(BF16)
    k = jnp.dot(h, wk_ref[...], preferred_element_type=F32)
    k_ref[...] = (_group_rms(k, gsum_ref, head_dim) * kg_ref[...]).astype(BF16)
    v_ref[...] = jnp.dot(h, wv_ref[...], preferred_element_type=F32).astype(BF16)


def _kv_proj(x, g, wk2, wv2, kg2, gsum, *, head_dim, tm):
    t, d = x.shape
    n = wk2.shape[1]
    fixed = lambda i: (0, 0)
    return pl.pallas_call(
        functools.partial(_kv_kernel, head_dim=head_dim),
        grid=(t // tm,),
        in_specs=[
            pl.BlockSpec((tm, d), lambda i: (i, 0)),
            pl.BlockSpec((1, d), fixed),
            pl.BlockSpec((d, n), fixed),
            pl.BlockSpec((d, n), fixed),
            pl.BlockSpec((1, n), fixed),
            pl.BlockSpec((MXU_COLS, MXU_COLS), fixed),
        ],
        out_specs=[pl.BlockSpec((tm, n), lambda i: (i, 0)), pl.BlockSpec((tm, n), lambda i: (i, 0))],
        out_shape=[jax.ShapeDtypeStruct((t, n), BF16), jax.ShapeDtypeStruct((t, n), BF16)],
        compiler_params=_params(
            ("parallel",), [((tm, d), F32), ((d, n), BF16), ((d, n), BF16), ((tm, n), BF16), ((tm, n), BF16)]),
        name="kv_proj",
    )(x, g, wk2, wv2, kg2, gsum)


def _q_kernel(x_ref, g_ref, w_ref, qg_ref, gsum_ref, o_ref, h_ref, *, head_dim):
    @pl.when(pl.program_id(1) == 0)
    def _():
        h_ref[...] = _rms_rows(x_ref[...], g_ref[...]).astype(BF16)

    q = jnp.dot(h_ref[...], w_ref[...], preferred_element_type=F32)
    qn = _group_rms(q, gsum_ref, head_dim) * qg_ref[...]
    o_ref[...] = (qn * (1.0 / math.sqrt(head_dim))).astype(BF16)


def _q_proj(x, g, w, layer, qg_t, gsum, *, head_dim, tm, tn):
    t, d = x.shape
    n = w.shape[2]
    return pl.pallas_call(
        functools.partial(_q_kernel, head_dim=head_dim),
        grid=(t // tm, n // tn),
        in_specs=[
            pl.BlockSpec((tm, d), lambda i, j: (i, 0)),
            pl.BlockSpec((1, d), lambda i, j: (0, 0)),
            pl.BlockSpec((None, d, tn), lambda i, j: (layer, 0, j)),
            pl.BlockSpec((1, tn), lambda i, j: (0, j)),
            pl.BlockSpec((MXU_COLS, MXU_COLS), lambda i, j: (0, 0)),
        ],
        out_specs=pl.BlockSpec((tm, tn), lambda i, j: (i, j)),
        out_shape=jax.ShapeDtypeStruct((t, n), BF16),
        scratch_shapes=[pltpu.VMEM((tm, d), BF16)],
        compiler_params=_params(
            ("parallel", "arbitrary"), [((tm, d), F32), ((d, tn), BF16), ((tm, tn), BF16)],
            [((tm, d), BF16)]),
        name="q_proj",
    )(x, g, w, qg_t, gsum)


def _attn_kernel(sink_ref, q_ref, k_ref, v_ref, o_ref, sc_ref, p_ref, es_ref, *,
                 n_kv, group, head_dim, q_blocks):
    blk = WINDOW
    inflight = sc_ref.shape[0]
    pairs = group // 2
    pair_w = 2 * head_dim
    lane = lax.broadcasted_iota(jnp.int32, (blk, pair_w), 1)
    lo_half = lane < head_dim
    row = lax.broadcasted_iota(jnp.int32, (blk, blk), 0)
    col = lax.broadcasted_iota(jnp.int32, (blk, blk), 1)
    from_cur = col <= row
    cur16 = jnp.where(from_cur, 1.0, 0.0).astype(BF16)
    prev16 = jnp.where(from_cur, 0.0, 1.0).astype(BF16)
    lo_half32 = lane.astype(F32) < float(head_dim)
    zero = jnp.zeros((), BF16)

    def halves(prev, cur):
        return jnp.concatenate([
            jnp.where(lo_half, prev, zero), jnp.where(lo_half, cur, zero),
            jnp.where(lo_half, zero, prev), jnp.where(lo_half, zero, cur)], axis=0)

    lo16 = jnp.where(lo_half32, 1.0, 0.0).astype(BF16)
    hi16 = jnp.where(lo_half32, 0.0, 1.0).astype(BF16)
    ones_cat = jnp.concatenate([lo16, lo16, hi16, hi16], axis=0)

    def phase1(sub, n, r_q):
        r_cur = pl.multiple_of(n * blk, blk)
        r_prev = pl.multiple_of(jnp.maximum(n - 1, 0) * blk, blk)
        prev_bias = jnp.where(n > 0, 0.0, NEG_INF)
        for h in range(n_kv):
            hs = slice(h * pair_w, (h + 1) * pair_w)
            kcat = halves(k_ref[pl.ds(r_prev, blk), hs], k_ref[pl.ds(r_cur, blk), hs])
            qstack = jnp.concatenate(
                [q_ref[pl.ds(r_q, blk), (h * pairs + p) * pair_w:(h * pairs + p + 1) * pair_w]
                 for p in range(pairs)], axis=0)
            s = lax.dot_general(qstack, kcat, (((1,), (1,)), ((), ())),
                                preferred_element_type=F32)
            for p in range(pairs):
                for e in range(2):
                    s_prev = s[p * blk:(p + 1) * blk, (2 * e) * blk:(2 * e + 1) * blk] + prev_bias
                    s_cur = s[p * blk:(p + 1) * blk, (2 * e + 1) * blk:(2 * e + 2) * blk]
                    sc_ref[sub, (h * pairs + p) * 2 + e] = jnp.where(from_cur, s_cur, s_prev)

    def phase2(sub):
        for h in range(n_kv):
            for p in range(pairs):
                ms = []
                for e in range(2):
                    sc = sc_ref[sub, (h * pairs + p) * 2 + e]
                    m = jnp.max(sc, axis=-1, keepdims=True)
                    pexp = jnp.exp(sc - m).astype(BF16)
                    p_ref[sub, h, p * blk:(p + 1) * blk, (2 * e) * blk:(2 * e + 1) * blk] = (
                        pexp * prev16)
                    p_ref[sub, h, p * blk:(p + 1) * blk, (2 * e + 1) * blk:(2 * e + 2) * blk] = (
                        pexp * cur16)
                    ms.append(sink_ref[h * group + 2 * p + e] - m)
                es_ref[sub, h * pairs + p] = jnp.exp(jnp.where(lo_half32, ms[0], ms[1]))

    def phase3(sub, n, r_q):
        r_cur = pl.multiple_of(n * blk, blk)
        r_prev = pl.multiple_of(jnp.maximum(n - 1, 0) * blk, blk)
        for h in range(n_kv):
            hs = slice(h * pair_w, (h + 1) * pair_w)
            vcat = halves(v_ref[pl.ds(r_prev, blk), hs], v_ref[pl.ds(r_cur, blk), hs])
            pv = jnp.dot(p_ref[sub, h], jnp.concatenate([vcat, ones_cat], axis=1),
                         preferred_element_type=F32)
            for p in range(pairs):
                num = pv[p * blk:(p + 1) * blk, :pair_w]
                den = pv[p * blk:(p + 1) * blk, pair_w:] + es_ref[sub, h * pairs + p]
                o_ref[pl.ds(r_q, blk), (h * pairs + p) * pair_w:(h * pairs + p + 1) * pair_w] = (
                    num * (1.0 / den)).astype(BF16)

    def block_body(it, _):
        blocks = []
        for sub in range(inflight):
            qb = it * inflight + sub
            n = pl.program_id(1) * q_blocks + qb
            blocks.append((sub, n, pl.multiple_of(qb * blk, blk)))
        for sub, n, r_q in blocks:
            phase1(sub, n, r_q)
        for sub, n, r_q in blocks:
            phase2(sub)
        for sub, n, r_q in blocks:
            phase3(sub, n, r_q)
        return 0

    lax.fori_loop(0, q_blocks // inflight, block_body, 0)


def _attention(sinks, q, k2, v2, *, batch, n_kv, group, head_dim, q_blocks):
    t, dq = q.shape
    s = t // batch
    nb = s // WINDOW
    inflight = 2 if q_blocks % 2 == 0 else 1
    assert nb % q_blocks == 0
    kw = k2.shape[1]
    n_heads = n_kv * group
    steps = nb // q_blocks
    qrow = lambda b, i: (b * steps + i, 0)
    whole = lambda b, i: (b, 0)
    return pl.pallas_call(
        functools.partial(_attn_kernel, n_kv=n_kv, group=group, head_dim=head_dim,
                          q_blocks=q_blocks),
        grid=(batch, steps),
        in_specs=[
            pl.BlockSpec(memory_space=pltpu.SMEM),
            pl.BlockSpec((q_blocks * WINDOW, dq), qrow),
            pl.BlockSpec((s, kw), whole),
            pl.BlockSpec((s, kw), whole),
        ],
        out_specs=pl.BlockSpec((q_blocks * WINDOW, dq), qrow),
        out_shape=jax.ShapeDtypeStruct((t, dq), BF16),
        scratch_shapes=[
            pltpu.VMEM((inflight, n_heads, WINDOW, WINDOW), F32),
            pltpu.VMEM((inflight, n_kv, (group // 2) * WINDOW, 4 * WINDOW), BF16),
            pltpu.VMEM((inflight, n_heads // 2, WINDOW, 2 * head_dim), F32),
        ],
        compiler_params=_params(
            ("parallel", "arbitrary"),
            [((q_blocks * WINDOW, dq), BF16)] * 2 + [((s, kw), BF16)] * 2,
            [((inflight, n_heads, WINDOW, WINDOW), F32), ((inflight, n_heads, WINDOW, WINDOW), BF16),
             ((inflight, n_heads // 2, WINDOW, 2 * head_dim), F32)]),
        name="swa_attention",
    )(sinks, q, k2, v2)


def _tile(n, target):
    if n <= target:
        return n
    best = None
    for c in range(128, target + 1, 128):
        if n % c == 0:
            best = c
    assert best is not None, (n, target)
    return best


def _dup_heads(w, n_kv, head_dim):
    d = w.shape[0]
    w3 = w.reshape(d, n_kv, head_dim)
    return jnp.concatenate([w3, w3], axis=-1).reshape(d, n_kv * 2 * head_dim)


def kernel(x, norm1_g, norm2_g, ffn_w_in, ffn_w_out, lru_w_in, lru_conv_w, lru_conv_b, lru_w_rg,
           lru_b_rg, lru_w_ig, lru_b_ig, lru_lambda, lru_w_out, kv_norm_g, w_kv, k_norm_g, w_q,
           q_norm_g, sinks, w_o):
    batch, seq, d = x.shape
    t = batch * seq
    depth = norm1_g.shape[0]
    n_rec = lru_w_in.shape[0]
    head_dim = k_norm_g.shape[0]
    n_kv = w_kv.shape[1] // (2 * head_dim)
    n_heads = w_q.shape[2] // head_dim
    group = n_heads // n_kv
    assert seq % WINDOW == 0 and group % 2 == 0 and 2 * head_dim == 128
    assert MXU_COLS % head_dim == 0 and (n_kv * 2 * head_dim) % MXU_COLS == 0

    tm = _tile(t, 1024)
    tn = _tile(d, 1024)
    f = ffn_w_out.shape[1]
    tf = _tile(f, 512)
    tc = _tile(seq, 256)

    row = lambda v: v.reshape(1, -1).astype(F32)
    gidx = jnp.arange(MXU_COLS) // head_dim
    gsum = (gidx[:, None] == gidx[None, :]).astype(BF16)

    lru_w_in16, lru_w_out16 = lru_w_in.astype(BF16), lru_w_out.astype(BF16)
    lru_w_rg16, lru_w_ig16 = lru_w_rg.astype(BF16), lru_w_ig.astype(BF16)
    w_q16, w_o16 = w_q.astype(BF16), w_o.astype(BF16)
    chain = _ffn_cast_ok(t, d, f, tm, tf)
    first_in_scan = chain and n_rec > 0 and _rglru_cast_ok(batch, seq // tc, d, f)
    if first_in_scan:
        ffn_w16 = None
    elif chain:
        ffn_w16 = (ffn_w_in[:1].astype(BF16), ffn_w_out[:1].astype(BF16))
    else:
        ffn_w16 = (ffn_w_in.astype(BF16), ffn_w_out.astype(BF16))

    xs = x.reshape(t, d)
    k2 = v2 = None
    for layer in range(depth):
        if layer < n_rec:
            i = layer
            gy, xbr = _rec_in_proj(xs, row(norm1_g[layer]), lru_w_in16, i, tm=tm, tn=tn)
            cast_w = (ffn_w_in, ffn_w_out, 0) if first_in_scan and layer == 0 else None
            outs = _rglru(xbr, gy, lru_conv_w[i], row(lru_conv_b[i]), lru_w_rg16, row(lru_b_rg[i]),
                          lru_w_ig16, row(lru_b_ig[i]), row(lru_lambda[i]), i, cast_w,
                          batch=batch, tc=tc, lane_chunk=_tile(d, 512))
            m = outs[0]
            if cast_w is not None:
                ffn_w16 = tuple(outs[1:])
            xs = _proj_residual(m, lru_w_out16, i, xs, tm=_tile(t, 512), tn=d)
        else:
            if layer == n_rec:
                wk, wv = jnp.split(w_kv, 2, axis=-1)
                k2, v2 = _kv_proj(
                    xs, row(kv_norm_g), _dup_heads(wk, n_kv, head_dim).astype(BF16),
                    _dup_heads(wv, n_kv, head_dim).astype(BF16),
                    row(jnp.tile(k_norm_g, 2 * n_kv)), gsum, head_dim=head_dim, tm=_tile(t, 512))
            j = layer - n_rec
            q = _q_proj(xs, row(norm1_g[layer]), w_q16, j, row(jnp.tile(q_norm_g[j], n_heads)),
                        gsum, head_dim=head_dim, tm=tm, tn=tn)
            o = _attention(sinks[j].astype(F32), q, k2, v2, batch=batch, n_kv=n_kv, group=group,
                           head_dim=head_dim, q_blocks=min(4, seq // WINDOW))
            xs = _proj_residual(o, w_o16, j, xs, tm=_tile(t, 512), tn=d)
        if chain:
            nxt = (ffn_w_in, ffn_w_out, layer + 1) if layer + 1 < depth else None
            outs = _ffn(xs, row(norm2_g[layer]), ffn_w16[0], ffn_w16[1], 0, nxt, tm=tm, tf=tf)
            xs, ffn_w16 = outs[0], tuple(outs[1:])
        else:
            xs = _ffn(xs, row(norm2_g[layer]), ffn_w16[0], ffn_w16[1], layer, tm=tm, tf=tf)[0]
    return xs.reshape(batch, seq, d)
```

```python
import functools
import math

import jax
import jax.numpy as jnp
from jax import lax
from jax.experimental import pallas as pl
from jax.experimental.pallas import tpu as pltpu

NORM_EPS = 1e-6
LRU_C = 8.0
WINDOW = 128
NEG_INF = -1e30
TINY_F32 = 1e-37
LOG2_E = 1.4426950408889634
MXU_COLS = 256
VMEM_CAP_BYTES = 58 * 1024 * 1024
COMPILER_TEMP_BYTES = 8 * 1024 * 1024

F32 = jnp.float32
BF16 = jnp.bfloat16


def _nbytes(shape, dtype):
    return math.prod(shape) * jnp.dtype(dtype).itemsize


def _params(sem, windows, scratch=()):
    need = sum(2 * _nbytes(*w) for w in windows) + sum(_nbytes(*b) for b in scratch)
    limit = min(need + COMPILER_TEMP_BYTES, VMEM_CAP_BYTES)
    return pltpu.CompilerParams(dimension_semantics=sem, vmem_limit_bytes=limit)


def _rms_rows(x, g):
    ms = jnp.mean(x * x, axis=-1, keepdims=True)
    return (x * lax.rsqrt(ms + NORM_EPS)) * g


def _gelu_tanh(x):
    c = math.sqrt(2.0 / math.pi)
    return 0.5 * x * (1.0 + jnp.tanh(c * (x + 0.044715 * (x * x * x))))


def _group_rms(q, gsum_ref, group):
    cols = q.shape[-1]
    chunk = min(cols, MXU_COLS)
    outs = []
    for c in range(cols // chunk):
        qc = q[:, c * chunk:(c + 1) * chunk]
        ssq = jnp.dot((qc * qc).astype(BF16), gsum_ref[:chunk, :chunk], preferred_element_type=F32)
        outs.append(qc * lax.rsqrt(ssq * (1.0 / group) + NORM_EPS))
    return outs[0] if len(outs) == 1 else jnp.concatenate(outs, axis=-1)


def _repeat_heads(a, head_dim):
    pair_w = 2 * head_dim
    lane = lax.broadcasted_iota(jnp.int32, (a.shape[0], pair_w), 1)
    lo = lane < head_dim
    outs = []
    for c in range(a.shape[1] // pair_w):
        slab = a[:, c * pair_w:(c + 1) * pair_w]
        swapped = pltpu.roll(slab, head_dim, axis=1)
        outs += [jnp.where(lo, slab, swapped), jnp.where(lo, swapped, slab)]
    return jnp.concatenate(outs, axis=1)


def _rec_in_kernel(x_ref, g_ref, wy_ref, wx_ref, oy_ref, ox_ref, h_ref):
    @pl.when(pl.program_id(1) == 0)
    def _():
        h_ref[...] = _rms_rows(x_ref[...], g_ref[...]).astype(BF16)

    h = h_ref[...]
    y = jnp.dot(h, wy_ref[...], preferred_element_type=F32)
    oy_ref[...] = _gelu_tanh(y).astype(BF16)
    ox_ref[...] = jnp.dot(h, wx_ref[...], preferred_element_type=F32).astype(BF16)


def _rec_in_proj(x, g, w_in, layer, *, tm, tn):
    t, d = x.shape
    w = w_in.shape[2] // 2
    nj = w // tn
    return pl.pallas_call(
        _rec_in_kernel,
        grid=(t // tm, nj),
        in_specs=[
            pl.BlockSpec((tm, d), lambda i, j: (i, 0)),
            pl.BlockSpec((1, d), lambda i, j: (0, 0)),
            pl.BlockSpec((None, d, tn), lambda i, j: (layer, 0, j)),
            pl.BlockSpec((None, d, tn), lambda i, j: (layer, 0, j + nj)),
        ],
        out_specs=[
            pl.BlockSpec((tm, tn), lambda i, j: (i, j)),
            pl.BlockSpec((tm, tn), lambda i, j: (i, j)),
        ],
        out_shape=[jax.ShapeDtypeStruct((t, w), BF16), jax.ShapeDtypeStruct((t, w), BF16)],
        scratch_shapes=[pltpu.VMEM((tm, d), BF16)],
        compiler_params=_params(
            ("parallel", "arbitrary"),
            [((tm, d), F32), ((d, tn), BF16), ((d, tn), BF16), ((tm, tn), BF16), ((tm, tn), BF16)],
            [((tm, d), BF16)]),
        name="rec_in_proj",
    )(x, g, w_in, w_in)


def _rglru_kernel(xbr_ref, gy_ref, perm_ref, permt_ref, cw_ref, cb_ref, wrg_ref, brg_ref, wig_ref,
                  big_ref, lam_ref, *rest, lane_chunk, n_cast):
    o_ref = rest[n_cast]
    for src_ref, dst_ref in zip(rest[:n_cast], rest[n_cast + 1:2 * n_cast + 1]):
        dst_ref[...] = src_ref[...].astype(BF16)
    tail_ref, carry_ref, a_ref, u_ref, gp_ref, m_ref = rest[2 * n_cast + 1:]
    tc, d = xbr_ref.shape
    taps = cw_ref.shape[0]
    seg = tc // 8
    halo = taps - 1

    @pl.when(pl.program_id(1) == 0)
    def _():
        tail_ref[...] = jnp.zeros_like(tail_ref)
        carry_ref[...] = jnp.zeros_like(carry_ref)

    perm = perm_ref[...]
    x0 = jnp.dot(perm, xbr_ref[...], preferred_element_type=F32)
    gp_ref[...] = jnp.dot(perm, gy_ref[...], preferred_element_type=F32)

    first_seg = lax.broadcasted_iota(jnp.int32, (8, d), 0) == 0
    tail = tail_ref[...]
    wrapped = []
    for i in range(halo):
        cur = x0[(seg - halo + i) * 8:(seg - halo + i + 1) * 8]
        prv = tail[i * 8:(i + 1) * 8]
        wrapped.append(jnp.where(first_seg, pltpu.roll(prv, 1, axis=0), pltpu.roll(cur, 1, axis=0)))
    tail_ref[...] = x0[(seg - halo) * 8:]

    xb = x0 * cw_ref[taps - 1:taps, :] + cb_ref[...]
    for k in range(1, taps):
        xk = jnp.concatenate(wrapped[halo - k:] + [x0[:tc - 8 * k]], axis=0)
        xb = xb + xk * cw_ref[taps - 1 - k:taps - k, :]

    xb16 = xb.astype(BF16)
    nblk = wrg_ref.shape[0]
    bw = d // nblk
    for n in range(nblk):
        sl = slice(n * bw, (n + 1) * bw)
        xs = xb16[:, sl]
        r = jax.nn.sigmoid(jnp.dot(xs, wrg_ref[n], preferred_element_type=F32) + brg_ref[:, sl])
        i = jax.nn.sigmoid(jnp.dot(xs, wig_ref[n], preferred_element_type=F32) + big_ref[:, sl])
        nlam = -lam_ref[:, sl]
        softplus = jnp.maximum(nlam, 0.0) + jnp.log1p(jnp.exp(-jnp.abs(nlam)))
        z = r * (LRU_C * softplus)
        a = jnp.exp2(z * (-LOG2_E))
        one_minus_a2 = jnp.tanh(z) * (1.0 + a * a)
        root = one_minus_a2 * lax.rsqrt(jnp.maximum(one_minus_a2, TINY_F32))
        a_ref[:, sl] = a
        u_ref[:, sl] = root * (i * xb[:, sl])

    rowc = lax.broadcasted_iota(jnp.int32, (8, lane_chunk), 0)
    for c in range(d // lane_chunk):
        ls = slice(c * lane_chunk, (c + 1) * lane_chunk)

        h_end = u_ref[0:8, ls]
        e_end = a_ref[0:8, ls]
        for j in range(1, seg):
            av = a_ref[j * 8:(j + 1) * 8, ls]
            h_end = av * h_end + u_ref[j * 8:(j + 1) * 8, ls]
            e_end = av * e_end
            u_ref[j * 8:(j + 1) * 8, ls] = h_end
            a_ref[j * 8:(j + 1) * 8, ls] = e_end

        for s in (1, 2, 4):
            keep = rowc >= s
            e_prev = jnp.where(keep, pltpu.roll(e_end, s, axis=0), 1.0)
            h_prev = jnp.where(keep, pltpu.roll(h_end, s, axis=0), 0.0)
            h_end = e_end * h_prev + h_end
            e_end = e_end * e_prev
        state_in = carry_ref[:, ls]
        after = e_end * state_in + h_end
        seg_in = jnp.where(rowc == 0, state_in, pltpu.roll(after, 1, axis=0))
        carry_ref[:, ls] = jnp.broadcast_to(after[7:8, :], after.shape)
        seg_in2 = jnp.concatenate([seg_in, seg_in], axis=0)

        for jj in range(seg // 2):
            rs = slice(jj * 16, (jj + 1) * 16)
            h = u_ref[rs, ls] + a_ref[rs, ls] * seg_in2
            m_ref[rs, ls] = (h * gp_ref[rs, ls]).astype(BF16)

    o_ref[...] = jnp.dot(permt_ref[...], m_ref[...], preferred_element_type=F32).astype(BF16)


def _cast_split(rows, cols, batch, nt):
    if rows % nt == 0 and (rows // nt) % 16 == 0 and cols % batch == 0 and (cols // batch) % 128 == 0:
        return (rows // nt, cols // batch), lambda b, i: (i, b)
    if rows % batch == 0 and (rows // batch) % 16 == 0 and cols % nt == 0 and (cols // nt) % 128 == 0:
        return (rows // batch, cols // nt), lambda b, i: (b, i)
    return None


def _rglru(xbr, gy, conv_w, conv_b, w_rg, b_rg, w_ig, b_ig, lam, layer, casts=(), *,
           batch, tc, lane_chunk):
    t, d = xbr.shape
    s = t // batch
    nt = s // tc
    _, nblk, bw, _ = w_rg.shape
    taps = conv_w.shape[0]
    assert tc % 16 == 0 and taps - 1 <= tc // 8
    rows = jnp.arange(tc)
    src = (rows % 8) * (tc // 8) + rows // 8
    perm = (src[:, None] == rows[None, :]).astype(BF16)
    row = lambda b, i: (b * nt + i, 0)
    fixed2 = lambda b, i: (0, 0)
    fixed4 = lambda b, i: (layer, 0, 0, 0)
    in_specs = [
        pl.BlockSpec((tc, d), row),
        pl.BlockSpec((tc, d), row),
        pl.BlockSpec((tc, tc), fixed2),
        pl.BlockSpec((tc, tc), fixed2),
        pl.BlockSpec(conv_w.shape, fixed2),
        pl.BlockSpec((1, d), fixed2),
        pl.BlockSpec((None, nblk, bw, bw), fixed4),
        pl.BlockSpec((1, d), fixed2),
        pl.BlockSpec((None, nblk, bw, bw), fixed4),
        pl.BlockSpec((1, d), fixed2),
        pl.BlockSpec((1, d), fixed2),
    ]
    out_specs = [pl.BlockSpec((tc, d), row)]
    out_shape = [jax.ShapeDtypeStruct((t, d), BF16)]
    args = [xbr, gy, perm, perm.T, conv_w, conv_b, w_rg, b_rg, w_ig, b_ig, lam]
    windows = [((tc, d), BF16)] * 3 + [((tc, tc), BF16)] * 2 + [((nblk, bw, bw), BF16)] * 2
    for arr, first, count in casts:
        blk, where = _cast_split(arr.shape[1], arr.shape[2], batch, nt)
        assert first % count == 0
        in_specs.append(pl.BlockSpec((count,) + blk, lambda b, i, w=where, l=first // count: (l,) + w(b, i)))
    for arr, first, count in casts:
        blk, where = _cast_split(arr.shape[1], arr.shape[2], batch, nt)
        out_specs.append(pl.BlockSpec((count,) + blk, lambda b, i, w=where: (0,) + w(b, i)))
        out_shape.append(jax.ShapeDtypeStruct((count,) + arr.shape[1:], BF16))
        args.append(arr)
        windows += [((count,) + blk, F32), ((count,) + blk, BF16)]
    return pl.pallas_call(
        functools.partial(_rglru_kernel, lane_chunk=lane_chunk, n_cast=len(casts)),
        grid=(batch, nt),
        in_specs=in_specs,
        out_specs=out_specs,
        out_shape=out_shape,
        scratch_shapes=[
            pltpu.VMEM(((taps - 1) * 8, d), F32),
            pltpu.VMEM((8, d), F32),
            pltpu.VMEM((tc, d), F32),
            pltpu.VMEM((tc, d), F32),
            pltpu.VMEM((tc, d), F32),
            pltpu.VMEM((tc, d), BF16),
        ],
        compiler_params=_params(("parallel", "arbitrary"), windows,
                                [((tc, d), F32)] * 3 + [((tc, d), BF16)]),
        name="rglru_scan",
    )(*args)


def _proj_res_kernel(m_ref, w_ref, x_ref, o_ref):
    o_ref[...] = x_ref[...] + jnp.dot(m_ref[...], w_ref[...], preferred_element_type=F32)


def _proj_residual(m, w, layer, x, *, tm, tn):
    t, k = m.shape
    n = w.shape[2]
    return pl.pallas_call(
        _proj_res_kernel,
        grid=(t // tm, n // tn),
        in_specs=[
            pl.BlockSpec((tm, k), lambda i, j: (i, 0)),
            pl.BlockSpec((None, k, tn), lambda i, j: (layer, 0, j)),
            pl.BlockSpec((tm, tn), lambda i, j: (i, j)),
        ],
        out_specs=pl.BlockSpec((tm, tn), lambda i, j: (i, j)),
        out_shape=jax.ShapeDtypeStruct((t, n), F32),
        compiler_params=_params(
            ("parallel", "arbitrary"),
            [((tm, k), BF16), ((k, tn), BF16), ((tm, tn), F32), ((tm, tn), F32)]),
        name="proj_residual",
    )(m, w, x)


def _ffn_kernel(x_ref, g_ref, wg_ref, wu_ref, wo_ref, *rest, cast_next):
    if cast_next:
        nwi_ref, nwo_ref, o_ref, cwi_ref, cwo_ref, h_ref = rest
        cwi_ref[...] = nwi_ref[...].astype(BF16)
        cwo_ref[...] = nwo_ref[...].astype(BF16)
    else:
        o_ref, h_ref = rest

    @pl.when(pl.program_id(1) == 0)
    def _():
        x = x_ref[...]
        h_ref[...] = _rms_rows(x, g_ref[...]).astype(BF16)
        o_ref[...] = x

    h = h_ref[...]
    gate = jnp.dot(h, wg_ref[...], preferred_element_type=F32)
    up = jnp.dot(h, wu_ref[...], preferred_element_type=F32)
    act = ((gate * jax.nn.sigmoid(gate)) * up).astype(BF16)
    o_ref[...] += jnp.dot(act, wo_ref[...], preferred_element_type=F32)


def _ffn_cast_ok(t, d, f, tm, tf):
    ni, nf = t // tm, f // tf
    return (d % ni == 0 and (d // ni) % 128 == 0 and (2 * f) % nf == 0
            and (2 * f // nf) % 128 == 0)


def _ffn(x, g, w_in, w_out, layer, next_w=None, *, tm, tf):
    t, d = x.shape
    f = w_out.shape[1]
    ni, nf = t // tm, f // tf
    in_specs = [
        pl.BlockSpec((tm, d), lambda i, j: (i, 0)),
        pl.BlockSpec((1, d), lambda i, j: (0, 0)),
        pl.BlockSpec((None, d, tf), lambda i, j: (layer, 0, j)),
        pl.BlockSpec((None, d, tf), lambda i, j: (layer, 0, j + nf)),
        pl.BlockSpec((None, tf, d), lambda i, j: (layer, j, 0)),
    ]
    out_specs = [pl.BlockSpec((tm, d), lambda i, j: (i, 0))]
    out_shape = [jax.ShapeDtypeStruct((t, d), F32)]
    args = [x, g, w_in, w_in, w_out]
    windows = [((tm, d), F32)] * 2 + [((d, tf), BF16)] * 3
    if next_w is not None:
        nwi, nwo, nl = next_w
        ri, ci, co = d // ni, 2 * f // nf, d // ni
        in_specs += [pl.BlockSpec((None, ri, ci), lambda i, j: (nl, i, j)),
                     pl.BlockSpec((None, tf, co), lambda i, j: (nl, j, i))]
        out_specs += [pl.BlockSpec((None, ri, ci), lambda i, j: (0, i, j)),
                      pl.BlockSpec((None, tf, co), lambda i, j: (0, j, i))]
        out_shape += [jax.ShapeDtypeStruct((1, d, 2 * f), BF16), jax.ShapeDtypeStruct((1, f, d), BF16)]
        args += [nwi, nwo]
        windows += [((ri, ci), F32), ((tf, co), F32), ((ri, ci), BF16), ((tf, co), BF16)]
    return pl.pallas_call(
        functools.partial(_ffn_kernel, cast_next=next_w is not None),
        grid=(ni, nf),
        in_specs=in_specs,
        out_specs=out_specs,
        out_shape=out_shape,
        scratch_shapes=[pltpu.VMEM((tm, d), BF16)],
        compiler_params=_params(("parallel", "arbitrary"), windows, [((tm, d), BF16)]),
        name="swiglu_ffn",
    )(*args)


def _kv_kernel(x_ref, g_ref, wk_ref, wv_ref, kg_ref, gsum_ref, k_ref, v_ref, *, head_dim):
    h = _rms_rows(x_ref[...], g_ref[...]).astype(BF16)
    k = jnp.dot(h, wk_ref[...], preferred_element_type=F32)
    kn = _group_rms(k, gsum_ref, head_dim) * kg_ref[...]
    k_ref[...] = _repeat_heads(kn, head_dim).astype(BF16)
    v = jnp.dot(h, wv_ref[...], preferred_element_type=F32)
    v_ref[...] = _repeat_heads(v, head_dim).astype(BF16)


def _kv_proj(x, g, wk, wv, kg, gsum, *, head_dim, tm):
    t, d = x.shape
    nk = wk.shape[1]
    n = 2 * nk
    fixed = lambda i: (0, 0)
    return pl.pallas_call(
        functools.partial(_kv_kernel, head_dim=head_dim),
        grid=(t // tm,),
        in_specs=[
            pl.BlockSpec((tm, d), lambda i: (i, 0)),
            pl.BlockSpec((1, d), fixed),
            pl.BlockSpec((d, nk), fixed),
            pl.BlockSpec((d, nk), fixed),
            pl.BlockSpec((1, nk), fixed),
            pl.BlockSpec((MXU_COLS, MXU_COLS), fixed),
        ],
        out_specs=[pl.BlockSpec((tm, n), lambda i: (i, 0)), pl.BlockSpec((tm, n), lambda i: (i, 0))],
        out_shape=[jax.ShapeDtypeStruct((t, n), BF16), jax.ShapeDtypeStruct((t, n), BF16)],
        compiler_params=_params(
            ("parallel",), [((tm, d), F32), ((d, nk), BF16), ((d, nk), BF16), ((tm, n), BF16), ((tm, n), BF16)]),
        name="kv_proj",
    )(x, g, wk, wv, kg, gsum)


def _q_kernel(x_ref, g_ref, w_ref, qg_ref, gsum_ref, o_ref, h_ref, *, head_dim):
    @pl.when(pl.program_id(1) == 0)
    def _():
        h_ref[...] = _rms_rows(x_ref[...], g_ref[...]).astype(BF16)

    q = jnp.dot(h_ref[...], w_ref[...], preferred_element_type=F32)
    qn = _group_rms(q, gsum_ref, head_dim) * qg_ref[...]
    o_ref[...] = (qn * (1.0 / math.sqrt(head_dim))).astype(BF16)


def _q_proj(x, g, w, layer, qg_t, gsum, *, head_dim, tm, tn):
    t, d = x.shape
    n = w.shape[2]
    return pl.pallas_call(
        functools.partial(_q_kernel, head_dim=head_dim),
        grid=(t // tm, n // tn),
        in_specs=[
            pl.BlockSpec((tm, d), lambda i, j: (i, 0)),
            pl.BlockSpec((1, d), lambda i, j: (0, 0)),
            pl.BlockSpec((None, d, tn), lambda i, j: (layer, 0, j)),
            pl.BlockSpec((1, tn), lambda i, j: (0, j)),
            pl.BlockSpec((MXU_COLS, MXU_COLS), lambda i, j: (0, 0)),
        ],
        out_specs=pl.BlockSpec((tm, tn), lambda i, j: (i, j)),
        out_shape=jax.ShapeDtypeStruct((t, n), BF16),
        scratch_shapes=[pltpu.VMEM((tm, d), BF16)],
        compiler_params=_params(
            ("parallel", "arbitrary"), [((tm, d), F32), ((d, tn), BF16), ((tm, tn), BF16)],
            [((tm, d), BF16)]),
        name="q_proj",
    )(x, g, w, qg_t, gsum)


def _attn_kernel(sink_ref, q_ref, k_ref, v_ref, o_ref, sc_ref, p_ref, es_ref, *,
                 n_kv, group, head_dim, q_blocks):
    blk = WINDOW
    inflight = sc_ref.shape[0]
    pairs = group // 2
    pair_w = 2 * head_dim
    lane = lax.broadcasted_iota(jnp.int32, (blk, pair_w), 1)
    lo_half = lane < head_dim
    row = lax.broadcasted_iota(jnp.int32, (blk, blk), 0)
    col = lax.broadcasted_iota(jnp.int32, (blk, blk), 1)
    from_cur = col <= row
    cur16 = jnp.where(from_cur, 1.0, 0.0).astype(BF16)
    prev16 = jnp.where(from_cur, 0.0, 1.0).astype(BF16)
    lo_half32 = lane.astype(F32) < float(head_dim)
    zero = jnp.zeros((), BF16)

    def halves(prev, cur):
        return jnp.concatenate([
            jnp.where(lo_half, prev, zero), jnp.where(lo_half, cur, zero),
            jnp.where(lo_half, zero, prev), jnp.where(lo_half, zero, cur)], axis=0)

    lo16 = jnp.where(lo_half32, 1.0, 0.0).astype(BF16)
    hi16 = jnp.where(lo_half32, 0.0, 1.0).astype(BF16)
    ones_cat = jnp.concatenate([lo16, lo16, hi16, hi16], axis=0)

    def phase1(sub, n, r_q):
        r_cur = pl.multiple_of(n * blk, blk)
        r_prev = pl.multiple_of(jnp.maximum(n - 1, 0) * blk, blk)
        prev_bias = jnp.where(n > 0, 0.0, NEG_INF)
        for h in range(n_kv):
            hs = slice(h * pair_w, (h + 1) * pair_w)
            kcat = halves(k_ref[pl.ds(r_prev, blk), hs], k_ref[pl.ds(r_cur, blk), hs])
            qstack = jnp.concatenate(
                [q_ref[pl.ds(r_q, blk), (h * pairs + p) * pair_w:(h * pairs + p + 1) * pair_w]
                 for p in range(pairs)], axis=0)
            s = lax.dot_general(qstack, kcat, (((1,), (1,)), ((), ())),
                                preferred_element_type=F32)
            for p in range(pairs):
                for e in range(2):
                    s_prev = s[p * blk:(p + 1) * blk, (2 * e) * blk:(2 * e + 1) * blk] + prev_bias
                    s_cur = s[p * blk:(p + 1) * blk, (2 * e + 1) * blk:(2 * e + 2) * blk]
                    sc_ref[sub, (h * pairs + p) * 2 + e] = jnp.where(from_cur, s_cur, s_prev)

    def phase2(sub):
        for h in range(n_kv):
            for p in range(pairs):
                ms = []
                for e in range(2):
                    sc = sc_ref[sub, (h * pairs + p) * 2 + e]
                    m = jnp.max(sc, axis=-1, keepdims=True)
                    pexp = jnp.exp(sc - m).astype(BF16)
                    p_ref[sub, h, p * blk:(p + 1) * blk, (2 * e) * blk:(2 * e + 1) * blk] = (
                        pexp * prev16)
                    p_ref[sub, h, p * blk:(p + 1) * blk, (2 * e + 1) * blk:(2 * e + 2) * blk] = (
                        pexp * cur16)
                    ms.append(sink_ref[h * group + 2 * p + e] - m)
                es_ref[sub, h * pairs + p] = jnp.exp(jnp.where(lo_half32, ms[0], ms[1]))

    def phase3(sub, n, r_q):
        r_cur = pl.multiple_of(n * blk, blk)
        r_prev = pl.multiple_of(jnp.maximum(n - 1, 0) * blk, blk)
        for h in range(n_kv):
            hs = slice(h * pair_w, (h + 1) * pair_w)
            vcat = halves(v_ref[pl.ds(r_prev, blk), hs], v_ref[pl.ds(r_cur, blk), hs])
            pv = jnp.dot(p_ref[sub, h], jnp.concatenate([vcat, ones_cat], axis=1),
                         preferred_element_type=F32)
            for p in range(pairs):
                num = pv[p * blk:(p + 1) * blk, :pair_w]
                den = pv[p * blk:(p + 1) * blk, pair_w:] + es_ref[sub, h * pairs + p]
                o_ref[pl.ds(r_q, blk), (h * pairs + p) * pair_w:(h * pairs + p + 1) * pair_w] = (
                    num * (1.0 / den)).astype(BF16)

    def block_body(it, _):
        blocks = []
        for sub in range(inflight):
            qb = it * inflight + sub
            n = pl.program_id(1) * q_blocks + qb
            blocks.append((sub, n, pl.multiple_of(qb * blk, blk)))
        for sub, n, r_q in blocks:
            phase1(sub, n, r_q)
        for sub, n, r_q in blocks:
            phase2(sub)
        for sub, n, r_q in blocks:
            phase3(sub, n, r_q)
        return 0

    lax.fori_loop(0, q_blocks // inflight, block_body, 0)


def _attention(sinks, q, k2, v2, *, batch, n_kv, group, head_dim, q_blocks):
    t, dq = q.shape
    s = t // batch
    nb = s // WINDOW
    inflight = 2 if q_blocks % 2 == 0 else 1
    assert nb % q_blocks == 0
    kw = k2.shape[1]
    n_heads = n_kv * group
    steps = nb // q_blocks
    qrow = lambda b, i: (b * steps + i, 0)
    whole = lambda b, i: (b, 0)
    return pl.pallas_call(
        functools.partial(_attn_kernel, n_kv=n_kv, group=group, head_dim=head_dim,
                          q_blocks=q_blocks),
        grid=(batch, steps),
        in_specs=[
            pl.BlockSpec(memory_space=pltpu.SMEM),
            pl.BlockSpec((q_blocks * WINDOW, dq), qrow),
            pl.BlockSpec((s, kw), whole),
            pl.BlockSpec((s, kw), whole),
        ],
        out_specs=pl.BlockSpec((q_blocks * WINDOW, dq), qrow),
        out_shape=jax.ShapeDtypeStruct((t, dq), BF16),
        scratch_shapes=[
            pltpu.VMEM((inflight, n_heads, WINDOW, WINDOW), F32),
            pltpu.VMEM((inflight, n_kv, (group // 2) * WINDOW, 4 * WINDOW), BF16),
            pltpu.VMEM((inflight, n_heads // 2, WINDOW, 2 * head_dim), F32),
        ],
        compiler_params=_params(
            ("parallel", "arbitrary"),
            [((q_blocks * WINDOW, dq), BF16)] * 2 + [((s, kw), BF16)] * 2,
            [((inflight, n_heads, WINDOW, WINDOW), F32), ((inflight, n_heads, WINDOW, WINDOW), BF16),
             ((inflight, n_heads // 2, WINDOW, 2 * head_dim), F32)]),
        name="swa_attention",
    )(sinks, q, k2, v2)


def _tile(n, target):
    if n <= target:
        return n
    best = None
    for c in range(128, target + 1, 128):
        if n % c == 0:
            best = c
    assert best is not None, (n, target)
    return best


def kernel(x, norm1_g, norm2_g, ffn_w_in, ffn_w_out, lru_w_in, lru_conv_w, lru_conv_b, lru_w_rg,
           lru_b_rg, lru_w_ig, lru_b_ig, lru_lambda, lru_w_out, kv_norm_g, w_kv, k_norm_g, w_q,
           q_norm_g, sinks, w_o):
    batch, seq, d = x.shape
    t = batch * seq
    depth = norm1_g.shape[0]
    n_rec = lru_w_in.shape[0]
    head_dim = k_norm_g.shape[0]
    n_kv = w_kv.shape[1] // (2 * head_dim)
    n_heads = w_q.shape[2] // head_dim
    group = n_heads // n_kv
    assert seq % WINDOW == 0 and group % 2 == 0 and 2 * head_dim == 128
    assert MXU_COLS % head_dim == 0 and n_kv % 2 == 0

    tm = _tile(t, 1024)
    tn = _tile(d, 1024)
    f = ffn_w_out.shape[1]
    tf = _tile(f, 512)
    tc = _tile(seq, 256)

    row = lambda v: v.reshape(1, -1).astype(F32)
    gidx = jnp.arange(MXU_COLS) // head_dim
    gsum = (gidx[:, None] == gidx[None, :]).astype(BF16)

    lru_w_in16 = lru_w_in.astype(BF16)
    lru_w_rg16, lru_w_ig16 = lru_w_rg.astype(BF16), lru_w_ig.astype(BF16)
    chain = _ffn_cast_ok(t, d, f, tm, tf)
    late = {"lru_w_out": lru_w_out, "w_q": w_q, "w_o": w_o}
    if chain:
        late.update(ffn_w_in=ffn_w_in, ffn_w_out=ffn_w_out)
    in_scan = {k: v for k, v in late.items()
               if n_rec > 0 and _cast_split(v.shape[1], v.shape[2], batch, seq // tc)}
    w16 = {k: v.astype(BF16) for k, v in late.items() if k not in in_scan and not k.startswith("ffn")}
    if not chain:
        w16.update(ffn_w_in=ffn_w_in.astype(BF16), ffn_w_out=ffn_w_out.astype(BF16))
    elif "ffn_w_in" not in in_scan or "ffn_w_out" not in in_scan:
        in_scan.pop("ffn_w_in", None), in_scan.pop("ffn_w_out", None)
        w16.update(ffn_w_in=ffn_w_in[:1].astype(BF16), ffn_w_out=ffn_w_out[:1].astype(BF16))

    xs = x.reshape(t, d)
    k2 = v2 = None
    for layer in range(depth):
        if layer < n_rec:
            i = layer
            gy, xbr = _rec_in_proj(xs, row(norm1_g[layer]), lru_w_in16, i, tm=tm, tn=tn)
            casts = [(v, 0, 1 if k.startswith("ffn") else v.shape[0])
                     for k, v in in_scan.items()] if layer == 0 else []
            outs = _rglru(xbr, gy, lru_conv_w[i], row(lru_conv_b[i]), lru_w_rg16, row(lru_b_rg[i]),
                          lru_w_ig16, row(lru_b_ig[i]), row(lru_lambda[i]), i, casts,
                          batch=batch, tc=tc, lane_chunk=_tile(d, 512))
            m = outs[0]
            if casts:
                w16.update(zip(in_scan, outs[1:]))
            xs = _proj_residual(m, w16["lru_w_out"], i, xs, tm=_tile(t, 512), tn=d)
        else:
            if layer == n_rec:
                wk, wv = jnp.split(w_kv, 2, axis=-1)
                k2, v2 = _kv_proj(
                    xs, row(kv_norm_g), wk.astype(BF16), wv.astype(BF16),
                    row(jnp.tile(k_norm_g, n_kv)), gsum, head_dim=head_dim, tm=_tile(t, 512))
            j = layer - n_rec
            q = _q_proj(xs, row(norm1_g[layer]), w16["w_q"], j, row(jnp.tile(q_norm_g[j], n_heads)),
                        gsum, head_dim=head_dim, tm=tm, tn=tn)
            o = _attention(sinks[j].astype(F32), q, k2, v2, batch=batch, n_kv=n_kv, group=group,
                           head_dim=head_dim, q_blocks=min(8, seq // WINDOW))
            xs = _proj_residual(o, w16["w_o"], j, xs, tm=_tile(t, 512), tn=d)
        if chain:
            nxt = (ffn_w_in, ffn_w_out, layer + 1) if layer + 1 < depth else None
            outs = _ffn(xs, row(norm2_g[layer]), w16["ffn_w_in"], w16["ffn_w_out"], 0, nxt,
                        tm=tm, tf=tf)
            xs = outs[0]
            if nxt is not None:
                w16.update(ffn_w_in=outs[1], ffn_w_out=outs[2])
        else:
            xs = _ffn(xs, row(norm2_g[layer]), w16["ffn_w_in"], w16["ffn_w_out"], layer,
                      tm=tm, tf=tf)[0]
    return xs.reshape(batch, seq, d)
```

```python
import functools
import math

import jax
import jax.numpy as jnp
from jax import lax
from jax.experimental import pallas as pl
from jax.experimental.pallas import tpu as pltpu

NORM_EPS = 1e-6
LRU_C = 8.0
WINDOW = 128
NEG_INF = -1e30
TINY_F32 = 1e-37
LOG2_E = 1.4426950408889634
MXU_COLS = 256
VMEM_CAP_BYTES = 58 * 1024 * 1024
COMPILER_TEMP_BYTES = 8 * 1024 * 1024

F32 = jnp.float32
BF16 = jnp.bfloat16


def _nbytes(shape, dtype):
    return math.prod(shape) * jnp.dtype(dtype).itemsize


def _params(sem, windows, scratch=()):
    need = sum(2 * _nbytes(*w) for w in windows) + sum(_nbytes(*b) for b in scratch)
    limit = min(need + COMPILER_TEMP_BYTES, VMEM_CAP_BYTES)
    return pltpu.CompilerParams(dimension_semantics=sem, vmem_limit_bytes=limit)


def _rms_rows(x, g):
    ms = jnp.mean(x * x, axis=-1, keepdims=True)
    return (x * lax.rsqrt(ms + NORM_EPS)) * g


def _gelu_tanh(x):
    c = math.sqrt(2.0 / math.pi)
    return 0.5 * x * (1.0 + jnp.tanh(c * (x + 0.044715 * (x * x * x))))


def _group_rms(q, gsum_ref, group):
    cols = q.shape[-1]
    chunk = min(cols, MXU_COLS)
    outs = []
    for c in range(cols // chunk):
        qc = q[:, c * chunk:(c + 1) * chunk]
        ssq = jnp.dot((qc * qc).astype(BF16), gsum_ref[:chunk, :chunk], preferred_element_type=F32)
        outs.append(qc * lax.rsqrt(ssq * (1.0 / group) + NORM_EPS))
    return outs[0] if len(outs) == 1 else jnp.concatenate(outs, axis=-1)


def _repeat_heads(a, head_dim):
    pair_w = 2 * head_dim
    lane = lax.broadcasted_iota(jnp.int32, (a.shape[0], pair_w), 1)
    lo = lane < head_dim
    outs = []
    for c in range(a.shape[1] // pair_w):
        slab = a[:, c * pair_w:(c + 1) * pair_w]
        swapped = pltpu.roll(slab, head_dim, axis=1)
        outs += [jnp.where(lo, slab, swapped), jnp.where(lo, swapped, slab)]
    return jnp.concatenate(outs, axis=1)


def _rec_in_kernel(x_ref, g_ref, wy_ref, wx_ref, oy_ref, ox_ref, h_ref):
    @pl.when(pl.program_id(1) == 0)
    def _():
        h_ref[...] = _rms_rows(x_ref[...], g_ref[...]).astype(BF16)

    h = h_ref[...]
    y = jnp.dot(h, wy_ref[...], preferred_element_type=F32)
    oy_ref[...] = _gelu_tanh(y).astype(BF16)
    ox_ref[...] = jnp.dot(h, wx_ref[...], preferred_element_type=F32).astype(BF16)


def _rec_in_proj(x, g, w_in, layer, *, tm, tn):
    t, d = x.shape
    w = w_in.shape[2] // 2
    nj = w // tn
    return pl.pallas_call(
        _rec_in_kernel,
        grid=(t // tm, nj),
        in_specs=[
            pl.BlockSpec((tm, d), lambda i, j: (i, 0)),
            pl.BlockSpec((1, d), lambda i, j: (0, 0)),
            pl.BlockSpec((None, d, tn), lambda i, j: (layer, 0, j)),
            pl.BlockSpec((None, d, tn), lambda i, j: (layer, 0, j + nj)),
        ],
        out_specs=[
            pl.BlockSpec((tm, tn), lambda i, j: (i, j)),
            pl.BlockSpec((tm, tn), lambda i, j: (i, j)),
        ],
        out_shape=[jax.ShapeDtypeStruct((t, w), BF16), jax.ShapeDtypeStruct((t, w), BF16)],
        scratch_shapes=[pltpu.VMEM((tm, d), BF16)],
        compiler_params=_params(
            ("parallel", "arbitrary"),
            [((tm, d), F32), ((d, tn), BF16), ((d, tn), BF16), ((tm, tn), BF16), ((tm, tn), BF16)],
            [((tm, d), BF16)]),
        name="rec_in_proj",
    )(x, g, w_in, w_in)


def _rglru_kernel(xbr_ref, gy_ref, perm_ref, permt_ref, cw_ref, cb_ref, wrg_ref, brg_ref, wig_ref,
                  big_ref, lam_ref, *rest, lane_chunk, n_cast):
    o_ref = rest[n_cast]
    for src_ref, dst_ref in zip(rest[:n_cast], rest[n_cast + 1:2 * n_cast + 1]):
        dst_ref[...] = src_ref[...].astype(BF16)
    tail_ref, carry_ref, a_ref, u_ref, gp_ref, m_ref = rest[2 * n_cast + 1:]
    tc, d = xbr_ref.shape
    taps = cw_ref.shape[0]
    seg = tc // 8
    halo = taps - 1

    @pl.when(pl.program_id(1) == 0)
    def _():
        tail_ref[...] = jnp.zeros_like(tail_ref)
        carry_ref[...] = jnp.zeros_like(carry_ref)

    perm = perm_ref[...]
    x0 = jnp.dot(perm, xbr_ref[...], preferred_element_type=F32)
    gp_ref[...] = jnp.dot(perm, gy_ref[...], preferred_element_type=F32)

    first_seg = lax.broadcasted_iota(jnp.int32, (8, d), 0) == 0
    tail = tail_ref[...]
    wrapped = []
    for i in range(halo):
        cur = x0[(seg - halo + i) * 8:(seg - halo + i + 1) * 8]
        prv = tail[i * 8:(i + 1) * 8]
        wrapped.append(jnp.where(first_seg, pltpu.roll(prv, 1, axis=0), pltpu.roll(cur, 1, axis=0)))
    tail_ref[...] = x0[(seg - halo) * 8:]

    cw_half = 0.5 * cw_ref[...]
    xh = x0 * cw_half[taps - 1:taps, :] + 0.5 * cb_ref[...]
    for k in range(1, taps):
        xk = jnp.concatenate(wrapped[halo - k:] + [x0[:tc - 8 * k]], axis=0)
        xh = xh + xk * cw_half[taps - 1 - k:taps - k, :]

    xh16 = xh.astype(BF16)
    nblk = wrg_ref.shape[0]
    bw = d // nblk
    for n in range(nblk):
        sl = slice(n * bw, (n + 1) * bw)
        xs = xh16[:, sl]
        tr = jnp.tanh(jnp.dot(xs, wrg_ref[n], preferred_element_type=F32) + 0.5 * brg_ref[:, sl])
        ti = jnp.tanh(jnp.dot(xs, wig_ref[n], preferred_element_type=F32) + 0.5 * big_ref[:, sl])
        nlam = -lam_ref[:, sl]
        softplus = jnp.maximum(nlam, 0.0) + jnp.log1p(jnp.exp(-jnp.abs(nlam)))
        half_c = (0.5 * LRU_C) * softplus
        z = tr * half_c + half_c
        a = jnp.exp2(z * (-LOG2_E))
        one_minus_a2 = jnp.tanh(z) * (1.0 + a * a)
        root = one_minus_a2 * lax.rsqrt(jnp.maximum(one_minus_a2, TINY_F32))
        a_ref[:, sl] = a
        u_ref[:, sl] = (root * xh[:, sl]) * (ti + 1.0)

    rowc = lax.broadcasted_iota(jnp.int32, (8, lane_chunk), 0)
    for c in range(d // lane_chunk):
        ls = slice(c * lane_chunk, (c + 1) * lane_chunk)

        h_end = u_ref[0:8, ls]
        e_end = a_ref[0:8, ls]
        for j in range(1, seg):
            av = a_ref[j * 8:(j + 1) * 8, ls]
            h_end = av * h_end + u_ref[j * 8:(j + 1) * 8, ls]
            e_end = av * e_end
            u_ref[j * 8:(j + 1) * 8, ls] = h_end
            a_ref[j * 8:(j + 1) * 8, ls] = e_end

        for s in (1, 2, 4):
            keep = rowc >= s
            e_prev = jnp.where(keep, pltpu.roll(e_end, s, axis=0), 1.0)
            h_prev = jnp.where(keep, pltpu.roll(h_end, s, axis=0), 0.0)
            h_end = e_end * h_prev + h_end
            e_end = e_end * e_prev
        state_in = carry_ref[:, ls]
        after = e_end * state_in + h_end
        seg_in = jnp.where(rowc == 0, state_in, pltpu.roll(after, 1, axis=0))
        carry_ref[:, ls] = jnp.broadcast_to(after[7:8, :], after.shape)
        seg_in2 = jnp.concatenate([seg_in, seg_in], axis=0)

        for jj in range(seg // 2):
            rs = slice(jj * 16, (jj + 1) * 16)
            h = u_ref[rs, ls] + a_ref[rs, ls] * seg_in2
            m_ref[rs, ls] = (h * gp_ref[rs, ls]).astype(BF16)

    o_ref[...] = jnp.dot(permt_ref[...], m_ref[...], preferred_element_type=F32).astype(BF16)


def _cast_split(rows, cols, batch, nt):
    if rows % nt == 0 and (rows // nt) % 16 == 0 and cols % batch == 0 and (cols // batch) % 128 == 0:
        return (rows // nt, cols // batch), lambda b, i: (i, b)
    if rows % batch == 0 and (rows // batch) % 16 == 0 and cols % nt == 0 and (cols // nt) % 128 == 0:
        return (rows // batch, cols // nt), lambda b, i: (b, i)
    return None


def _rglru(xbr, gy, conv_w, conv_b, w_rg, b_rg, w_ig, b_ig, lam, layer, casts=(), *,
           batch, tc, lane_chunk):
    t, d = xbr.shape
    s = t // batch
    nt = s // tc
    _, nblk, bw, _ = w_rg.shape
    taps = conv_w.shape[0]
    assert tc % 16 == 0 and taps - 1 <= tc // 8
    rows = jnp.arange(tc)
    src = (rows % 8) * (tc // 8) + rows // 8
    perm = (src[:, None] == rows[None, :]).astype(BF16)
    row = lambda b, i: (b * nt + i, 0)
    fixed2 = lambda b, i: (0, 0)
    fixed4 = lambda b, i: (layer, 0, 0, 0)
    in_specs = [
        pl.BlockSpec((tc, d), row),
        pl.BlockSpec((tc, d), row),
        pl.BlockSpec((tc, tc), fixed2),
        pl.BlockSpec((tc, tc), fixed2),
        pl.BlockSpec(conv_w.shape, fixed2),
        pl.BlockSpec((1, d), fixed2),
        pl.BlockSpec((None, nblk, bw, bw), fixed4),
        pl.BlockSpec((1, d), fixed2),
        pl.BlockSpec((None, nblk, bw, bw), fixed4),
        pl.BlockSpec((1, d), fixed2),
        pl.BlockSpec((1, d), fixed2),
    ]
    out_specs = [pl.BlockSpec((tc, d), row)]
    out_shape = [jax.ShapeDtypeStruct((t, d), BF16)]
    args = [xbr, gy, perm, perm.T, conv_w, conv_b, w_rg, b_rg, w_ig, b_ig, lam]
    windows = [((tc, d), BF16)] * 3 + [((tc, tc), BF16)] * 2 + [((nblk, bw, bw), BF16)] * 2
    for arr, first, count in casts:
        blk, where = _cast_split(arr.shape[1], arr.shape[2], batch, nt)
        assert first % count == 0
        in_specs.append(pl.BlockSpec((count,) + blk, lambda b, i, w=where, l=first // count: (l,) + w(b, i)))
    for arr, first, count in casts:
        blk, where = _cast_split(arr.shape[1], arr.shape[2], batch, nt)
        out_specs.append(pl.BlockSpec((count,) + blk, lambda b, i, w=where: (0,) + w(b, i)))
        out_shape.append(jax.ShapeDtypeStruct((count,) + arr.shape[1:], BF16))
        args.append(arr)
        windows += [((count,) + blk, F32), ((count,) + blk, BF16)]
    return pl.pallas_call(
        functools.partial(_rglru_kernel, lane_chunk=lane_chunk, n_cast=len(casts)),
        grid=(batch, nt),
        in_specs=in_specs,
        out_specs=out_specs,
        out_shape=out_shape,
        scratch_shapes=[
            pltpu.VMEM(((taps - 1) * 8, d), F32),
            pltpu.VMEM((8, d), F32),
            pltpu.VMEM((tc, d), F32),
            pltpu.VMEM((tc, d), F32),
            pltpu.VMEM((tc, d), F32),
            pltpu.VMEM((tc, d), BF16),
        ],
        compiler_params=_params(("parallel", "arbitrary"), windows,
                                [((tc, d), F32)] * 3 + [((tc, d), BF16)]),
        name="rglru_scan",
    )(*args)


def _proj_res_kernel(m_ref, w_ref, x_ref, o_ref):
    o_ref[...] = x_ref[...] + jnp.dot(m_ref[...], w_ref[...], preferred_element_type=F32)


def _proj_residual(m, w, layer, x, *, tm, tn):
    t, k = m.shape
    n = w.shape[2]
    return pl.pallas_call(
        _proj_res_kernel,
        grid=(t // tm, n // tn),
        in_specs=[
            pl.BlockSpec((tm, k), lambda i, j: (i, 0)),
            pl.BlockSpec((None, k, tn), lambda i, j: (layer, 0, j)),
            pl.BlockSpec((tm, tn), lambda i, j: (i, j)),
        ],
        out_specs=pl.BlockSpec((tm, tn), lambda i, j: (i, j)),
        out_shape=jax.ShapeDtypeStruct((t, n), F32),
        compiler_params=_params(
            ("parallel", "arbitrary"),
            [((tm, k), BF16), ((k, tn), BF16), ((tm, tn), F32), ((tm, tn), F32)]),
        name="proj_residual",
    )(m, w, x)


def _ffn_kernel(x_ref, g_ref, wg_ref, wu_ref, wo_ref, *rest, cast_next):
    if cast_next:
        nwi_ref, nwo_ref, o_ref, cwi_ref, cwo_ref, h_ref = rest
        cwi_ref[...] = nwi_ref[...].astype(BF16)
        cwo_ref[...] = nwo_ref[...].astype(BF16)
    else:
        o_ref, h_ref = rest

    @pl.when(pl.program_id(1) == 0)
    def _():
        x = x_ref[...]
        h_ref[...] = _rms_rows(x, g_ref[...]).astype(BF16)
        o_ref[...] = x

    h = h_ref[...]
    gate = jnp.dot(h, wg_ref[...], preferred_element_type=F32)
    up = jnp.dot(h, wu_ref[...], preferred_element_type=F32)
    act = ((gate * jax.nn.sigmoid(gate)) * up).astype(BF16)
    o_ref[...] += jnp.dot(act, wo_ref[...], preferred_element_type=F32)


def _ffn_cast_ok(t, d, f, tm, tf):
    ni, nf = t // tm, f // tf
    return (d % ni == 0 and (d // ni) % 128 == 0 and (2 * f) % nf == 0
            and (2 * f // nf) % 128 == 0)


def _ffn(x, g, w_in, w_out, layer, next_w=None, *, tm, tf):
    t, d = x.shape
    f = w_out.shape[1]
    ni, nf = t // tm, f // tf
    in_specs = [
        pl.BlockSpec((tm, d), lambda i, j: (i, 0)),
        pl.BlockSpec((1, d), lambda i, j: (0, 0)),
        pl.BlockSpec((None, d, tf), lambda i, j: (layer, 0, j)),
        pl.BlockSpec((None, d, tf), lambda i, j: (layer, 0, j + nf)),
        pl.BlockSpec((None, tf, d), lambda i, j: (layer, j, 0)),
    ]
    out_specs = [pl.BlockSpec((tm, d), lambda i, j: (i, 0))]
    out_shape = [jax.ShapeDtypeStruct((t, d), F32)]
    args = [x, g, w_in, w_in, w_out]
    windows = [((tm, d), F32)] * 2 + [((d, tf), BF16)] * 3
    if next_w is not None:
        nwi, nwo, nl = next_w
        ri, ci, co = d // ni, 2 * f // nf, d // ni
        in_specs += [pl.BlockSpec((None, ri, ci), lambda i, j: (nl, i, j)),
                     pl.BlockSpec((None, tf, co), lambda i, j: (nl, j, i))]
        out_specs += [pl.BlockSpec((None, ri, ci), lambda i, j: (0, i, j)),
                      pl.BlockSpec((None, tf, co), lambda i, j: (0, j, i))]
        out_shape += [jax.ShapeDtypeStruct((1, d, 2 * f), BF16), jax.ShapeDtypeStruct((1, f, d), BF16)]
        args += [nwi, nwo]
        windows += [((ri, ci), F32), ((tf, co), F32), ((ri, ci), BF16), ((tf, co), BF16)]
    return pl.pallas_call(
        functools.partial(_ffn_kernel, cast_next=next_w is not None),
        grid=(ni, nf),
        in_specs=in_specs,
        out_specs=out_specs,
        out_shape=out_shape,
        scratch_shapes=[pltpu.VMEM((tm, d), BF16)],
        compiler_params=_params(("parallel", "arbitrary"), windows, [((tm, d), BF16)]),
        name="swiglu_ffn",
    )(*args)


def _kv_kernel(x_ref, g_ref, wk_ref, wv_ref, kg_ref, gsum_ref, k_ref, v_ref, *, head_dim):
    h = _rms_rows(x_ref[...], g_ref[...]).astype(BF16)
    k = jnp.dot(h, wk_ref[...], preferred_element_type=F32)
    kn = _group_rms(k, gsum_ref, head_dim) * kg_ref[...]
    k_ref[...] = _repeat_heads(kn, head_dim).astype(BF16)
    v = jnp.dot(h, wv_ref[...], preferred_element_type=F32)
    v_ref[...] = _repeat_heads(v, head_dim).astype(BF16)


def _kv_proj(x, g, wk, wv, kg, gsum, *, head_dim, tm):
    t, d = x.shape
    nk = wk.shape[1]
    n = 2 * nk
    fixed = lambda i: (0, 0)
    return pl.pallas_call(
        functools.partial(_kv_kernel, head_dim=head_dim),
        grid=(t // tm,),
        in_specs=[
            pl.BlockSpec((tm, d), lambda i: (i, 0)),
            pl.BlockSpec((1, d), fixed),
            pl.BlockSpec((d, nk), fixed),
            pl.BlockSpec((d, nk), fixed),
            pl.BlockSpec((1, nk), fixed),
            pl.BlockSpec((MXU_COLS, MXU_COLS), fixed),
        ],
        out_specs=[pl.BlockSpec((tm, n), lambda i: (i, 0)), pl.BlockSpec((tm, n), lambda i: (i, 0))],
        out_shape=[jax.ShapeDtypeStruct((t, n), BF16), jax.ShapeDtypeStruct((t, n), BF16)],
        compiler_params=_params(
            ("parallel",), [((tm, d), F32), ((d, nk), BF16), ((d, nk), BF16), ((tm, n), BF16), ((tm, n), BF16)]),
        name="kv_proj",
    )(x, g, wk, wv, kg, gsum)


def _q_kernel(x_ref, g_ref, w_ref, qg_ref, gsum_ref, o_ref, h_ref, *, head_dim):
    @pl.when(pl.program_id(1) == 0)
    def _():
        h_ref[...] = _rms_rows(x_ref[...], g_ref[...]).astype(BF16)

    q = jnp.dot(h_ref[...], w_ref[...], preferred_element_type=F32)
    qn = _group_rms(q, gsum_ref, head_dim) * qg_ref[...]
    o_ref[...] = (qn * (1.0 / math.sqrt(head_dim))).astype(BF16)


def _q_proj(x, g, w, layer, qg_t, gsum, *, head_dim, tm, tn):
    t, d = x.shape
    n = w.shape[2]
    return pl.pallas_call(
        functools.partial(_q_kernel, head_dim=head_dim),
        grid=(t // tm, n // tn),
        in_specs=[
            pl.BlockSpec((tm, d), lambda i, j: (i, 0)),
            pl.BlockSpec((1, d), lambda i, j: (0, 0)),
            pl.BlockSpec((None, d, tn), lambda i, j: (layer, 0, j)),
            pl.BlockSpec((1, tn), lambda i, j: (0, j)),
            pl.BlockSpec((MXU_COLS, MXU_COLS), lambda i, j: (0, 0)),
        ],
        out_specs=pl.BlockSpec((tm, tn), lambda i, j: (i, j)),
        out_shape=jax.ShapeDtypeStruct((t, n), BF16),
        scratch_shapes=[pltpu.VMEM((tm, d), BF16)],
        compiler_params=_params(
            ("parallel", "arbitrary"), [((tm, d), F32), ((d, tn), BF16), ((tm, tn), BF16)],
            [((tm, d), BF16)]),
        name="q_proj",
    )(x, g, w, qg_t, gsum)


def _attn_kernel(sink_ref, q_ref, k_ref, v_ref, o_ref, p_ref, es_ref, *,
                 n_kv, group, head_dim, q_blocks):
    blk = WINDOW
    inflight = p_ref.shape[0]
    pairs = group // 2
    pair_w = 2 * head_dim
    lane = lax.broadcasted_iota(jnp.int32, (blk, pair_w), 1)
    lo_half = lane < head_dim
    row = lax.broadcasted_iota(jnp.int32, (blk, blk), 0)
    col = lax.broadcasted_iota(jnp.int32, (blk, blk), 1)
    from_cur = col <= row
    cur16 = jnp.where(from_cur, 1.0, 0.0).astype(BF16)
    prev16 = jnp.where(from_cur, 0.0, 1.0).astype(BF16)
    lo_half32 = lane.astype(F32) < float(head_dim)
    zero = jnp.zeros((), BF16)

    def halves(prev, cur):
        return jnp.concatenate([
            jnp.where(lo_half, prev, zero), jnp.where(lo_half, cur, zero),
            jnp.where(lo_half, zero, prev), jnp.where(lo_half, zero, cur)], axis=0)

    lo16 = jnp.where(lo_half32, 1.0, 0.0).astype(BF16)
    hi16 = jnp.where(lo_half32, 0.0, 1.0).astype(BF16)
    ones_cat = jnp.concatenate([lo16, lo16, hi16, hi16], axis=0)

    def phase1(sub, n, r_q):
        r_cur = pl.multiple_of(n * blk, blk)
        r_prev = pl.multiple_of(jnp.maximum(n - 1, 0) * blk, blk)
        prev_bias = jnp.where(n > 0, 0.0, NEG_INF)
        for h in range(n_kv):
            hs = slice(h * pair_w, (h + 1) * pair_w)
            kcat = halves(k_ref[pl.ds(r_prev, blk), hs], k_ref[pl.ds(r_cur, blk), hs])
            qstack = jnp.concatenate(
                [q_ref[pl.ds(r_q, blk), (h * pairs + p) * pair_w:(h * pairs + p + 1) * pair_w]
                 for p in range(pairs)], axis=0)
            s = lax.dot_general(qstack, kcat, (((1,), (1,)), ((), ())),
                                preferred_element_type=F32)
            for p in range(pairs):
                ms = []
                for e in range(2):
                    s_prev = s[p * blk:(p + 1) * blk, (2 * e) * blk:(2 * e + 1) * blk] + prev_bias
                    s_cur = s[p * blk:(p + 1) * blk, (2 * e + 1) * blk:(2 * e + 2) * blk]
                    sc = jnp.where(from_cur, s_cur, s_prev)
                    m = jnp.max(sc, axis=-1, keepdims=True)
                    pexp = jnp.exp(sc - m).astype(BF16)
                    p_ref[sub, h, p * blk:(p + 1) * blk, (2 * e) * blk:(2 * e + 1) * blk] = (
                        pexp * prev16)
                    p_ref[sub, h, p * blk:(p + 1) * blk, (2 * e + 1) * blk:(2 * e + 2) * blk] = (
                        pexp * cur16)
                    ms.append(sink_ref[h * group + 2 * p + e] - m)
                es_ref[sub, h * pairs + p] = jnp.exp(jnp.where(lo_half32, ms[0], ms[1]))

    def phase3(sub, n, r_q):
        r_cur = pl.multiple_of(n * blk, blk)
        r_prev = pl.multiple_of(jnp.maximum(n - 1, 0) * blk, blk)
        for h in range(n_kv):
            hs = slice(h * pair_w, (h + 1) * pair_w)
            vcat = halves(v_ref[pl.ds(r_prev, blk), hs], v_ref[pl.ds(r_cur, blk), hs])
            pv = jnp.dot(p_ref[sub, h], jnp.concatenate([vcat, ones_cat], axis=1),
                         preferred_element_type=F32)
            for p in range(pairs):
                num = pv[p * blk:(p + 1) * blk, :pair_w]
                den = pv[p * blk:(p + 1) * blk, pair_w:] + es_ref[sub, h * pairs + p]
                o_ref[pl.ds(r_q, blk), (h * pairs + p) * pair_w:(h * pairs + p + 1) * pair_w] = (
                    num * (1.0 / den)).astype(BF16)

    def block_body(it, _):
        blocks = []
        for sub in range(inflight):
            qb = it * inflight + sub
            n = pl.program_id(1) * q_blocks + qb
            blocks.append((sub, n, pl.multiple_of(qb * blk, blk)))
        for sub, n, r_q in blocks:
            phase1(sub, n, r_q)
        for sub, n, r_q in blocks:
            phase3(sub, n, r_q)
        return 0

    lax.fori_loop(0, q_blocks // inflight, block_body, 0)


def _attention(sinks, q, k2, v2, *, batch, n_kv, group, head_dim, q_blocks):
    t, dq = q.shape
    s = t // batch
    nb = s // WINDOW
    inflight = 2 if q_blocks % 2 == 0 else 1
    assert nb % q_blocks == 0
    kw = k2.shape[1]
    n_heads = n_kv * group
    steps = nb // q_blocks
    qrow = lambda b, i: (b * steps + i, 0)
    whole = lambda b, i: (b, 0)
    return pl.pallas_call(
        functools.partial(_attn_kernel, n_kv=n_kv, group=group, head_dim=head_dim,
                          q_blocks=q_blocks),
        grid=(batch, steps),
        in_specs=[
            pl.BlockSpec(memory_space=pltpu.SMEM),
            pl.BlockSpec((q_blocks * WINDOW, dq), qrow),
            pl.BlockSpec((s, kw), whole),
            pl.BlockSpec((s, kw), whole),
        ],
        out_specs=pl.BlockSpec((q_blocks * WINDOW, dq), qrow),
        out_shape=jax.ShapeDtypeStruct((t, dq), BF16),
        scratch_shapes=[
            pltpu.VMEM((inflight, n_kv, (group // 2) * WINDOW, 4 * WINDOW), BF16),
            pltpu.VMEM((inflight, n_heads // 2, WINDOW, 2 * head_dim), F32),
        ],
        compiler_params=_params(
            ("parallel", "arbitrary"),
            [((q_blocks * WINDOW, dq), BF16)] * 2 + [((s, kw), BF16)] * 2,
            [((inflight, n_heads, WINDOW, WINDOW), BF16),
             ((inflight, n_heads // 2, WINDOW, 2 * head_dim), F32)]),
        name="swa_attention",
    )(sinks, q, k2, v2)


def _tile(n, target):
    if n <= target:
        return n
    best = None
    for c in range(128, target + 1, 128):
        if n % c == 0:
            best = c
    assert best is not None, (n, target)
    return best


def kernel(x, norm1_g, norm2_g, ffn_w_in, ffn_w_out, lru_w_in, lru_conv_w, lru_conv_b, lru_w_rg,
           lru_b_rg, lru_w_ig, lru_b_ig, lru_lambda, lru_w_out, kv_norm_g, w_kv, k_norm_g, w_q,
           q_norm_g, sinks, w_o):
    batch, seq, d = x.shape
    t = batch * seq
    depth = norm1_g.shape[0]
    n_rec = lru_w_in.shape[0]
    head_dim = k_norm_g.shape[0]
    n_kv = w_kv.shape[1] // (2 * head_dim)
    n_heads = w_q.shape[2] // head_dim
    group = n_heads // n_kv
    assert seq % WINDOW == 0 and group % 2 == 0 and 2 * head_dim == 128
    assert MXU_COLS % head_dim == 0 and n_kv % 2 == 0

    tm = _tile(t, 1024)
    tn = _tile(d, 1024)
    f = ffn_w_out.shape[1]
    tf = _tile(f, 512)
    tc = _tile(seq, 256)

    row = lambda v: v.reshape(1, -1).astype(F32)
    gidx = jnp.arange(MXU_COLS) // head_dim
    gsum = (gidx[:, None] == gidx[None, :]).astype(BF16)

    lru_w_in16 = lru_w_in.astype(BF16)
    lru_w_rg16, lru_w_ig16 = lru_w_rg.astype(BF16), lru_w_ig.astype(BF16)
    chain = _ffn_cast_ok(t, d, f, tm, tf)
    late = {"lru_w_out": lru_w_out, "w_q": w_q, "w_o": w_o}
    if chain:
        late.update(ffn_w_in=ffn_w_in, ffn_w_out=ffn_w_out)
    in_scan = {k: v for k, v in late.items()
               if n_rec > 0 and _cast_split(v.shape[1], v.shape[2], batch, seq // tc)}
    w16 = {k: v.astype(BF16) for k, v in late.items() if k not in in_scan and not k.startswith("ffn")}
    if not chain:
        w16.update(ffn_w_in=ffn_w_in.astype(BF16), ffn_w_out=ffn_w_out.astype(BF16))
    elif "ffn_w_in" not in in_scan or "ffn_w_out" not in in_scan:
        in_scan.pop("ffn_w_in", None), in_scan.pop("ffn_w_out", None)
        w16.update(ffn_w_in=ffn_w_in[:1].astype(BF16), ffn_w_out=ffn_w_out[:1].astype(BF16))

    xs = x.reshape(t, d)
    k2 = v2 = None
    for layer in range(depth):
        if layer < n_rec:
            i = layer
            gy, xbr = _rec_in_proj(xs, row(norm1_g[layer]), lru_w_in16, i, tm=tm, tn=tn)
            casts = [(v, 0, 1 if k.startswith("ffn") else v.shape[0])
                     for k, v in in_scan.items()] if layer == 0 else []
            outs = _rglru(xbr, gy, lru_conv_w[i], row(lru_conv_b[i]), lru_w_rg16, row(lru_b_rg[i]),
                          lru_w_ig16, row(lru_b_ig[i]), row(lru_lambda[i]), i, casts,
                          batch=batch, tc=tc, lane_chunk=_tile(d, 512))
            m = outs[0]
            if casts:
                w16.update(zip(in_scan, outs[1:]))
            xs = _proj_residual(m, w16["lru_w_out"], i, xs, tm=_tile(t, 512), tn=d)
        else:
            if layer == n_rec:
                wk, wv = jnp.split(w_kv, 2, axis=-1)
                k2, v2 = _kv_proj(
                    xs, row(kv_norm_g), wk.astype(BF16), wv.astype(BF16),
                    row(jnp.tile(k_norm_g, n_kv)), gsum, head_dim=head_dim, tm=_tile(t, 512))
            j = layer - n_rec
            q = _q_proj(xs, row(norm1_g[layer]), w16["w_q"], j, row(jnp.tile(q_norm_g[j], n_heads)),
                        gsum, head_dim=head_dim, tm=tm, tn=tn)
            o = _attention(sinks[j].astype(F32), q, k2, v2, batch=batch, n_kv=n_kv, group=group,
                           head_dim=head_dim, q_blocks=min(8, seq // WINDOW))
            xs = _proj_residual(o, w16["w_o"], j, xs, tm=_tile(t, 512), tn=d)
        if chain:
            nxt = (ffn_w_in, ffn_w_out, layer + 1) if layer + 1 < depth else None
            outs = _ffn(xs, row(norm2_g[layer]), w16["ffn_w_in"], w16["ffn_w_out"], 0, nxt,
                        tm=tm, tf=tf)
            xs = outs[0]
            if nxt is not None:
                w16.update(ffn_w_in=outs[1], ffn_w_out=outs[2])
        else:
            xs = _ffn(xs, row(norm2_g[layer]), w16["ffn_w_in"], w16["ffn_w_out"], layer,
                      tm=tm, tf=tf)[0]
    return xs.reshape(batch, seq, d)
```

```python
import functools
import math

import jax
import jax.numpy as jnp
from jax import lax
from jax.experimental import pallas as pl
from jax.experimental.pallas import tpu as pltpu

NORM_EPS = 1e-6
LRU_C = 8.0
WINDOW = 128
NEG_INF = -1e30
TINY_F32 = 1e-37
LOG2_E = 1.4426950408889634
MXU_COLS = 256
VMEM_CAP_BYTES = 58 * 1024 * 1024
COMPILER_TEMP_BYTES = 8 * 1024 * 1024

F32 = jnp.float32
BF16 = jnp.bfloat16


def _nbytes(shape, dtype):
    return math.prod(shape) * jnp.dtype(dtype).itemsize


def _params(sem, windows, scratch=()):
    need = sum(2 * _nbytes(*w) for w in windows) + sum(_nbytes(*b) for b in scratch)
    limit = min(need + COMPILER_TEMP_BYTES, VMEM_CAP_BYTES)
    return pltpu.CompilerParams(dimension_semantics=sem, vmem_limit_bytes=limit)


def _rms_rows(x, g):
    ms = jnp.mean(x * x, axis=-1, keepdims=True)
    return (x * lax.rsqrt(ms + NORM_EPS)) * g


def _gelu_tanh(x):
    c = math.sqrt(2.0 / math.pi)
    return 0.5 * x * (1.0 + jnp.tanh(c * (x + 0.044715 * (x * x * x))))


def _group_rms(q, gsum_ref, group):
    cols = q.shape[-1]
    chunk = min(cols, MXU_COLS)
    outs = []
    for c in range(cols // chunk):
        qc = q[:, c * chunk:(c + 1) * chunk]
        ssq = jnp.dot((qc * qc).astype(BF16), gsum_ref[:chunk, :chunk], preferred_element_type=F32)
        outs.append(qc * lax.rsqrt(ssq * (1.0 / group) + NORM_EPS))
    return outs[0] if len(outs) == 1 else jnp.concatenate(outs, axis=-1)


def _repeat_heads(a, head_dim):
    pair_w = 2 * head_dim
    lane = lax.broadcasted_iota(jnp.int32, (a.shape[0], pair_w), 1)
    lo = lane < head_dim
    outs = []
    for c in range(a.shape[1] // pair_w):
        slab = a[:, c * pair_w:(c + 1) * pair_w]
        swapped = pltpu.roll(slab, head_dim, axis=1)
        outs += [jnp.where(lo, slab, swapped), jnp.where(lo, swapped, slab)]
    return jnp.concatenate(outs, axis=1)


def _rec_in_kernel(x_ref, g_ref, wy_ref, wx_ref, oy_ref, ox_ref, *scratch, one_step):
    if one_step:
        h = _rms_rows(x_ref[...], g_ref[...]).astype(BF16)
    else:
        (h_ref,) = scratch

        @pl.when(pl.program_id(1) == 0)
        def _():
            h_ref[...] = _rms_rows(x_ref[...], g_ref[...]).astype(BF16)

        h = h_ref[...]
    y = jnp.dot(h, wy_ref[...], preferred_element_type=F32)
    oy_ref[...] = _gelu_tanh(y).astype(BF16)
    ox_ref[...] = jnp.dot(h, wx_ref[...], preferred_element_type=F32).astype(BF16)


def _rec_in_proj(x, g, w_in, layer, *, tm, tn):
    t, d = x.shape
    w = w_in.shape[2] // 2
    nj = w // tn
    h_scratch = [] if nj == 1 else [((tm, d), BF16)]
    return pl.pallas_call(
        functools.partial(_rec_in_kernel, one_step=nj == 1),
        grid=(t // tm, nj),
        in_specs=[
            pl.BlockSpec((tm, d), lambda i, j: (i, 0)),
            pl.BlockSpec((1, d), lambda i, j: (0, 0)),
            pl.BlockSpec((None, d, tn), lambda i, j: (layer, 0, j)),
            pl.BlockSpec((None, d, tn), lambda i, j: (layer, 0, j + nj)),
        ],
        out_specs=[
            pl.BlockSpec((tm, tn), lambda i, j: (i, j)),
            pl.BlockSpec((tm, tn), lambda i, j: (i, j)),
        ],
        out_shape=[jax.ShapeDtypeStruct((t, w), BF16), jax.ShapeDtypeStruct((t, w), BF16)],
        scratch_shapes=[pltpu.VMEM(*b) for b in h_scratch],
        compiler_params=_params(
            ("parallel", "arbitrary"),
            [((tm, d), F32), ((d, tn), BF16), ((d, tn), BF16), ((tm, tn), BF16), ((tm, tn), BF16)],
            h_scratch),
        name="rec_in_proj",
    )(x, g, w_in, w_in)


def _rglru_kernel(xbr_ref, gy_ref, perm_ref, permt_ref, cw_ref, cb_ref, wrg_ref, brg_ref, wig_ref,
                  big_ref, lam_ref, *rest, lane_chunk, n_cast):
    o_ref = rest[n_cast]
    for src_ref, dst_ref in zip(rest[:n_cast], rest[n_cast + 1:2 * n_cast + 1]):
        dst_ref[...] = src_ref[...].astype(BF16)
    tail_ref, carry_ref, a_ref, u_ref, gp_ref, m_ref = rest[2 * n_cast + 1:]
    tc, d = xbr_ref.shape
    taps = cw_ref.shape[0]
    seg = tc // 8
    halo = taps - 1

    @pl.when(pl.program_id(1) == 0)
    def _():
        tail_ref[...] = jnp.zeros_like(tail_ref)
        carry_ref[...] = jnp.zeros_like(carry_ref)

    perm = perm_ref[...]
    x0 = jnp.dot(perm, xbr_ref[...], preferred_element_type=F32)
    gp_ref[...] = jnp.dot(perm, gy_ref[...], preferred_element_type=F32)

    first_seg = lax.broadcasted_iota(jnp.int32, (8, d), 0) == 0
    tail = tail_ref[...]
    wrapped = []
    for i in range(halo):
        cur = x0[(seg - halo + i) * 8:(seg - halo + i + 1) * 8]
        prv = tail[i * 8:(i + 1) * 8]
        wrapped.append(jnp.where(first_seg, pltpu.roll(prv, 1, axis=0), pltpu.roll(cur, 1, axis=0)))
    tail_ref[...] = x0[(seg - halo) * 8:]

    cw_half = 0.5 * cw_ref[...]
    xh = x0 * cw_half[taps - 1:taps, :] + 0.5 * cb_ref[...]
    for k in range(1, taps):
        xk = jnp.concatenate(wrapped[halo - k:] + [x0[:tc - 8 * k]], axis=0)
        xh = xh + xk * cw_half[taps - 1 - k:taps - k, :]

    xh16 = xh.astype(BF16)
    nblk = wrg_ref.shape[0]
    bw = d // nblk
    for n in range(nblk):
        sl = slice(n * bw, (n + 1) * bw)
        xs = xh16[:, sl]
        tr = jnp.tanh(jnp.dot(xs, wrg_ref[n], preferred_element_type=F32) + 0.5 * brg_ref[:, sl])
        ti = jnp.tanh(jnp.dot(xs, wig_ref[n], preferred_element_type=F32) + 0.5 * big_ref[:, sl])
        nlam = -lam_ref[:, sl]
        softplus = jnp.maximum(nlam, 0.0) + jnp.log1p(jnp.exp(-jnp.abs(nlam)))
        half_c = (0.5 * LRU_C) * softplus
        z = tr * half_c + half_c
        a = jnp.exp2(z * (-LOG2_E))
        one_minus_a2 = jnp.tanh(z) * (1.0 + a * a)
        root = one_minus_a2 * lax.rsqrt(jnp.maximum(one_minus_a2, TINY_F32))
        a_ref[:, sl] = a
        u_ref[:, sl] = (root * xh[:, sl]) * (ti + 1.0)

    rowc = lax.broadcasted_iota(jnp.int32, (8, lane_chunk), 0)
    for c in range(d // lane_chunk):
        ls = slice(c * lane_chunk, (c + 1) * lane_chunk)

        h_end = u_ref[0:8, ls]
        e_end = a_ref[0:8, ls]
        for j in range(1, seg):
            av = a_ref[j * 8:(j + 1) * 8, ls]
            h_end = av * h_end + u_ref[j * 8:(j + 1) * 8, ls]
            e_end = av * e_end
            u_ref[j * 8:(j + 1) * 8, ls] = h_end
            a_ref[j * 8:(j + 1) * 8, ls] = e_end

        for s in (1, 2, 4):
            keep = rowc >= s
            e_prev = jnp.where(keep, pltpu.roll(e_end, s, axis=0), 1.0)
            h_prev = jnp.where(keep, pltpu.roll(h_end, s, axis=0), 0.0)
            h_end = e_end * h_prev + h_end
            e_end = e_end * e_prev
        state_in = carry_ref[:, ls]
        after = e_end * state_in + h_end
        seg_in = jnp.where(rowc == 0, state_in, pltpu.roll(after, 1, axis=0))
        carry_ref[:, ls] = jnp.broadcast_to(after[7:8, :], after.shape)
        seg_in2 = jnp.concatenate([seg_in, seg_in], axis=0)

        for jj in range(seg // 2):
            rs = slice(jj * 16, (jj + 1) * 16)
            h = u_ref[rs, ls] + a_ref[rs, ls] * seg_in2
            m_ref[rs, ls] = (h * gp_ref[rs, ls]).astype(BF16)

    o_ref[...] = jnp.dot(permt_ref[...], m_ref[...], preferred_element_type=F32).astype(BF16)


def _cast_split(rows, cols, batch, nt):
    if rows % nt == 0 and (rows // nt) % 16 == 0 and cols % batch == 0 and (cols // batch) % 128 == 0:
        return (rows // nt, cols // batch), lambda b, i: (i, b)
    if rows % batch == 0 and (rows // batch) % 16 == 0 and cols % nt == 0 and (cols // nt) % 128 == 0:
        return (rows // batch, cols // nt), lambda b, i: (b, i)
    return None


def _rglru(xbr, gy, conv_w, conv_b, w_rg, b_rg, w_ig, b_ig, lam, layer, casts=(), *,
           batch, tc, lane_chunk):
    t, d = xbr.shape
    s = t // batch
    nt = s // tc
    _, nblk, bw, _ = w_rg.shape
    taps = conv_w.shape[0]
    assert tc % 16 == 0 and taps - 1 <= tc // 8
    rows = jnp.arange(tc)
    src = (rows % 8) * (tc // 8) + rows // 8
    perm = (src[:, None] == rows[None, :]).astype(BF16)
    row = lambda b, i: (b * nt + i, 0)
    fixed2 = lambda b, i: (0, 0)
    fixed4 = lambda b, i: (layer, 0, 0, 0)
    in_specs = [
        pl.BlockSpec((tc, d), row),
        pl.BlockSpec((tc, d), row),
        pl.BlockSpec((tc, tc), fixed2),
        pl.BlockSpec((tc, tc), fixed2),
        pl.BlockSpec(conv_w.shape, fixed2),
        pl.BlockSpec((1, d), fixed2),
        pl.BlockSpec((None, nblk, bw, bw), fixed4),
        pl.BlockSpec((1, d), fixed2),
        pl.BlockSpec((None, nblk, bw, bw), fixed4),
        pl.BlockSpec((1, d), fixed2),
        pl.BlockSpec((1, d), fixed2),
    ]
    out_specs = [pl.BlockSpec((tc, d), row)]
    out_shape = [jax.ShapeDtypeStruct((t, d), BF16)]
    args = [xbr, gy, perm, perm.T, conv_w, conv_b, w_rg, b_rg, w_ig, b_ig, lam]
    windows = [((tc, d), BF16)] * 3 + [((tc, tc), BF16)] * 2 + [((nblk, bw, bw), BF16)] * 2
    for arr, first, count in casts:
        blk, where = _cast_split(arr.shape[1], arr.shape[2], batch, nt)
        assert first % count == 0
        in_specs.append(pl.BlockSpec((count,) + blk, lambda b, i, w=where, l=first // count: (l,) + w(b, i)))
    for arr, first, count in casts:
        blk, where = _cast_split(arr.shape[1], arr.shape[2], batch, nt)
        out_specs.append(pl.BlockSpec((count,) + blk, lambda b, i, w=where: (0,) + w(b, i)))
        out_shape.append(jax.ShapeDtypeStruct((count,) + arr.shape[1:], BF16))
        args.append(arr)
        windows += [((count,) + blk, F32), ((count,) + blk, BF16)]
    return pl.pallas_call(
        functools.partial(_rglru_kernel, lane_chunk=lane_chunk, n_cast=len(casts)),
        grid=(batch, nt),
        in_specs=in_specs,
        out_specs=out_specs,
        out_shape=out_shape,
        scratch_shapes=[
            pltpu.VMEM(((taps - 1) * 8, d), F32),
            pltpu.VMEM((8, d), F32),
            pltpu.VMEM((tc, d), F32),
            pltpu.VMEM((tc, d), F32),
            pltpu.VMEM((tc, d), F32),
            pltpu.VMEM((tc, d), BF16),
        ],
        compiler_params=_params(("parallel", "arbitrary"), windows,
                                [((tc, d), F32)] * 3 + [((tc, d), BF16)]),
        name="rglru_scan",
    )(*args)


def _proj_res_kernel(m_ref, w_ref, x_ref, o_ref):
    o_ref[...] = x_ref[...] + jnp.dot(m_ref[...], w_ref[...], preferred_element_type=F32)


def _proj_residual(m, w, layer, x, *, tm, tn):
    t, k = m.shape
    n = w.shape[2]
    return pl.pallas_call(
        _proj_res_kernel,
        grid=(t // tm, n // tn),
        in_specs=[
            pl.BlockSpec((tm, k), lambda i, j: (i, 0)),
            pl.BlockSpec((None, k, tn), lambda i, j: (layer, 0, j)),
            pl.BlockSpec((tm, tn), lambda i, j: (i, j)),
        ],
        out_specs=pl.BlockSpec((tm, tn), lambda i, j: (i, j)),
        out_shape=jax.ShapeDtypeStruct((t, n), F32),
        compiler_params=_params(
            ("parallel", "arbitrary"),
            [((tm, k), BF16), ((k, tn), BF16), ((tm, tn), F32), ((tm, tn), F32)]),
        name="proj_residual",
    )(m, w, x)


def _ffn_kernel(x_ref, g_ref, wg_ref, wu_ref, wo_ref, *rest, cast_next):
    if cast_next:
        nwi_ref, nwo_ref, o_ref, cwi_ref, cwo_ref, h_ref = rest
        cwi_ref[...] = nwi_ref[...].astype(BF16)
        cwo_ref[...] = nwo_ref[...].astype(BF16)
    else:
        o_ref, h_ref = rest

    @pl.when(pl.program_id(1) == 0)
    def _():
        x = x_ref[...]
        h_ref[...] = _rms_rows(x, g_ref[...]).astype(BF16)
        o_ref[...] = x

    h = h_ref[...]
    gate = jnp.dot(h, wg_ref[...], preferred_element_type=F32)
    up = jnp.dot(h, wu_ref[...], preferred_element_type=F32)
    act = ((gate * jax.nn.sigmoid(gate)) * up).astype(BF16)
    o_ref[...] += jnp.dot(act, wo_ref[...], preferred_element_type=F32)


def _ffn_cast_ok(t, d, f, tm, tf):
    ni, nf = t // tm, f // tf
    return (d % ni == 0 and (d // ni) % 128 == 0 and (2 * f) % nf == 0
            and (2 * f // nf) % 128 == 0)


def _ffn(x, g, w_in, w_out, layer, next_w=None, *, tm, tf):
    t, d = x.shape
    f = w_out.shape[1]
    ni, nf = t // tm, f // tf
    in_specs = [
        pl.BlockSpec((tm, d), lambda i, j: (i, 0)),
        pl.BlockSpec((1, d), lambda i, j: (0, 0)),
        pl.BlockSpec((None, d, tf), lambda i, j: (layer, 0, j)),
        pl.BlockSpec((None, d, tf), lambda i, j: (layer, 0, j + nf)),
        pl.BlockSpec((None, tf, d), lambda i, j: (layer, j, 0)),
    ]
    out_specs = [pl.BlockSpec((tm, d), lambda i, j: (i, 0))]
    out_shape = [jax.ShapeDtypeStruct((t, d), F32)]
    args = [x, g, w_in, w_in, w_out]
    windows = [((tm, d), F32)] * 2 + [((d, tf), BF16)] * 3
    if next_w is not None:
        nwi, nwo, nl = next_w
        ri, ci, co = d // ni, 2 * f // nf, d // ni
        in_specs += [pl.BlockSpec((None, ri, ci), lambda i, j: (nl, i, j)),
                     pl.BlockSpec((None, tf, co), lambda i, j: (nl, j, i))]
        out_specs += [pl.BlockSpec((None, ri, ci), lambda i, j: (0, i, j)),
                      pl.BlockSpec((None, tf, co), lambda i, j: (0, j, i))]
        out_shape += [jax.ShapeDtypeStruct((1, d, 2 * f), BF16), jax.ShapeDtypeStruct((1, f, d), BF16)]
        args += [nwi, nwo]
        windows += [((ri, ci), F32), ((tf, co), F32), ((ri, ci), BF16), ((tf, co), BF16)]
    return pl.pallas_call(
        functools.partial(_ffn_kernel, cast_next=next_w is not None),
        grid=(ni, nf),
        in_specs=in_specs,
        out_specs=out_specs,
        out_shape=out_shape,
        scratch_shapes=[pltpu.VMEM((tm, d), BF16)],
        compiler_params=_params(("parallel", "arbitrary"), windows, [((tm, d), BF16)]),
        name="swiglu_ffn",
    )(*args)


def _kv_kernel(x_ref, g_ref, wk_ref, wv_ref, kg_ref, gsum_ref, k_ref, v_ref, *, head_dim):
    h = _rms_rows(x_ref[...], g_ref[...]).astype(BF16)
    k = jnp.dot(h, wk_ref[...], preferred_element_type=F32)
    kn = _group_rms(k, gsum_ref, head_dim) * kg_ref[...]
    k_ref[...] = _repeat_heads(kn, head_dim).astype(BF16)
    v = jnp.dot(h, wv_ref[...], preferred_element_type=F32)
    v_ref[...] = _repeat_heads(v, head_dim).astype(BF16)


def _kv_proj(x, g, wk, wv, kg, gsum, *, head_dim, tm):
    t, d = x.shape
    nk = wk.shape[1]
    n = 2 * nk
    fixed = lambda i: (0, 0)
    return pl.pallas_call(
        functools.partial(_kv_kernel, head_dim=head_dim),
        grid=(t // tm,),
        in_specs=[
            pl.BlockSpec((tm, d), lambda i: (i, 0)),
            pl.BlockSpec((1, d), fixed),
            pl.BlockSpec((d, nk), fixed),
            pl.BlockSpec((d, nk), fixed),
            pl.BlockSpec((1, nk), fixed),
            pl.BlockSpec((MXU_COLS, MXU_COLS), fixed),
        ],
        out_specs=[pl.BlockSpec((tm, n), lambda i: (i, 0)), pl.BlockSpec((tm, n), lambda i: (i, 0))],
        out_shape=[jax.ShapeDtypeStruct((t, n), BF16), jax.ShapeDtypeStruct((t, n), BF16)],
        compiler_params=_params(
            ("parallel",), [((tm, d), F32), ((d, nk), BF16), ((d, nk), BF16), ((tm, n), BF16), ((tm, n), BF16)]),
        name="kv_proj",
    )(x, g, wk, wv, kg, gsum)


def _q_kernel(x_ref, g_ref, w_ref, qg_ref, gsum_ref, o_ref, *scratch, head_dim, one_step):
    if one_step:
        h = _rms_rows(x_ref[...], g_ref[...]).astype(BF16)
    else:
        (h_ref,) = scratch

        @pl.when(pl.program_id(1) == 0)
        def _():
            h_ref[...] = _rms_rows(x_ref[...], g_ref[...]).astype(BF16)

        h = h_ref[...]
    q = jnp.dot(h, w_ref[...], preferred_element_type=F32)
    qn = _group_rms(q, gsum_ref, head_dim) * qg_ref[...]
    o_ref[...] = (qn * (1.0 / math.sqrt(head_dim))).astype(BF16)


def _q_proj(x, g, w, layer, qg_t, gsum, *, head_dim, tm, tn):
    t, d = x.shape
    n = w.shape[2]
    h_scratch = [] if n == tn else [((tm, d), BF16)]
    return pl.pallas_call(
        functools.partial(_q_kernel, head_dim=head_dim, one_step=n == tn),
        grid=(t // tm, n // tn),
        in_specs=[
            pl.BlockSpec((tm, d), lambda i, j: (i, 0)),
            pl.BlockSpec((1, d), lambda i, j: (0, 0)),
            pl.BlockSpec((None, d, tn), lambda i, j: (layer, 0, j)),
            pl.BlockSpec((1, tn), lambda i, j: (0, j)),
            pl.BlockSpec((MXU_COLS, MXU_COLS), lambda i, j: (0, 0)),
        ],
        out_specs=pl.BlockSpec((tm, tn), lambda i, j: (i, j)),
        out_shape=jax.ShapeDtypeStruct((t, n), BF16),
        scratch_shapes=[pltpu.VMEM(*b) for b in h_scratch],
        compiler_params=_params(
            ("parallel", "arbitrary"), [((tm, d), F32), ((d, tn), BF16), ((tm, tn), BF16)],
            h_scratch),
        name="q_proj",
    )(x, g, w, qg_t, gsum)


def _attn_kernel(sink_ref, q_ref, k_ref, v_ref, o_ref, p_ref, es_ref, *,
                 n_kv, group, head_dim, q_blocks):
    blk = WINDOW
    inflight = p_ref.shape[0]
    pairs = group // 2
    pair_w = 2 * head_dim
    lane = lax.broadcasted_iota(jnp.int32, (blk, pair_w), 1)
    lo_half = lane < head_dim
    row = lax.broadcasted_iota(jnp.int32, (blk, blk), 0)
    col = lax.broadcasted_iota(jnp.int32, (blk, blk), 1)
    from_cur = col <= row
    cur16 = jnp.where(from_cur, 1.0, 0.0).astype(BF16)
    prev16 = jnp.where(from_cur, 0.0, 1.0).astype(BF16)
    lo_half32 = lane.astype(F32) < float(head_dim)
    zero = jnp.zeros((), BF16)

    def halves(prev, cur):
        return jnp.concatenate([
            jnp.where(lo_half, prev, zero), jnp.where(lo_half, cur, zero),
            jnp.where(lo_half, zero, prev), jnp.where(lo_half, zero, cur)], axis=0)

    lo16 = jnp.where(lo_half32, 1.0, 0.0).astype(BF16)
    hi16 = jnp.where(lo_half32, 0.0, 1.0).astype(BF16)
    ones_cat = jnp.concatenate([lo16, lo16, hi16, hi16], axis=0)

    def phase1(sub, n, r_q):
        r_cur = pl.multiple_of(n * blk, blk)
        r_prev = pl.multiple_of(jnp.maximum(n - 1, 0) * blk, blk)
        prev_bias = jnp.where(n > 0, 0.0, NEG_INF)
        for h in range(n_kv):
            hs = slice(h * pair_w, (h + 1) * pair_w)
            kcat = halves(k_ref[pl.ds(r_prev, blk), hs], k_ref[pl.ds(r_cur, blk), hs])
            qstack = jnp.concatenate(
                [q_ref[pl.ds(r_q, blk), (h * pairs + p) * pair_w:(h * pairs + p + 1) * pair_w]
                 for p in range(pairs)], axis=0)
            s = lax.dot_general(qstack, kcat, (((1,), (1,)), ((), ())),
                                preferred_element_type=F32)
            for p in range(pairs):
                ms = []
                for e in range(2):
                    s_prev = s[p * blk:(p + 1) * blk, (2 * e) * blk:(2 * e + 1) * blk] + prev_bias
                    s_cur = s[p * blk:(p + 1) * blk, (2 * e + 1) * blk:(2 * e + 2) * blk]
                    sc = jnp.where(from_cur, s_cur, s_prev)
                    m = jnp.max(sc, axis=-1, keepdims=True)
                    pexp = jnp.exp(sc - m).astype(BF16)
                    p_ref[sub, h, p * blk:(p + 1) * blk, (2 * e) * blk:(2 * e + 1) * blk] = (
                        pexp * prev16)
                    p_ref[sub, h, p * blk:(p + 1) * blk, (2 * e + 1) * blk:(2 * e + 2) * blk] = (
                        pexp * cur16)
                    ms.append(sink_ref[h * group + 2 * p + e] - m)
                es_ref[sub, h * pairs + p] = jnp.exp(jnp.where(lo_half32, ms[0], ms[1]))

    def phase3(sub, n, r_q):
        r_cur = pl.multiple_of(n * blk, blk)
        r_prev = pl.multiple_of(jnp.maximum(n - 1, 0) * blk, blk)
        for h in range(n_kv):
            hs = slice(h * pair_w, (h + 1) * pair_w)
            vcat = halves(v_ref[pl.ds(r_prev, blk), hs], v_ref[pl.ds(r_cur, blk), hs])
            pv = jnp.dot(p_ref[sub, h], jnp.concatenate([vcat, ones_cat], axis=1),
                         preferred_element_type=F32)
            for p in range(pairs):
                num = pv[p * blk:(p + 1) * blk, :pair_w]
                den = pv[p * blk:(p + 1) * blk, pair_w:] + es_ref[sub, h * pairs + p]
                o_ref[pl.ds(r_q, blk), (h * pairs + p) * pair_w:(h * pairs + p + 1) * pair_w] = (
                    num * (1.0 / den)).astype(BF16)

    def block_body(it, _):
        blocks = []
        for sub in range(inflight):
            qb = it * inflight + sub
            n = pl.program_id(1) * q_blocks + qb
            blocks.append((sub, n, pl.multiple_of(qb * blk, blk)))
        for sub, n, r_q in blocks:
            phase1(sub, n, r_q)
        for sub, n, r_q in blocks:
            phase3(sub, n, r_q)
        return 0

    lax.fori_loop(0, q_blocks // inflight, block_body, 0)


def _attention(sinks, q, k2, v2, *, batch, n_kv, group, head_dim, q_blocks):
    t, dq = q.shape
    s = t // batch
    nb = s // WINDOW
    inflight = 2 if q_blocks % 2 == 0 else 1
    assert nb % q_blocks == 0
    kw = k2.shape[1]
    n_heads = n_kv * group
    steps = nb // q_blocks
    qrow = lambda b, i: (b * steps + i, 0)
    whole = lambda b, i: (b, 0)
    return pl.pallas_call(
        functools.partial(_attn_kernel, n_kv=n_kv, group=group, head_dim=head_dim,
                          q_blocks=q_blocks),
        grid=(batch, steps),
        in_specs=[
            pl.BlockSpec(memory_space=pltpu.SMEM),
            pl.BlockSpec((q_blocks * WINDOW, dq), qrow),
            pl.BlockSpec((s, kw), whole),
            pl.BlockSpec((s, kw), whole),
        ],
        out_specs=pl.BlockSpec((q_blocks * WINDOW, dq), qrow),
        out_shape=jax.ShapeDtypeStruct((t, dq), BF16),
        scratch_shapes=[
            pltpu.VMEM((inflight, n_kv, (group // 2) * WINDOW, 4 * WINDOW), BF16),
            pltpu.VMEM((inflight, n_heads // 2, WINDOW, 2 * head_dim), F32),
        ],
        compiler_params=_params(
            ("parallel", "arbitrary"),
            [((q_blocks * WINDOW, dq), BF16)] * 2 + [((s, kw), BF16)] * 2,
            [((inflight, n_heads, WINDOW, WINDOW), BF16),
             ((inflight, n_heads // 2, WINDOW, 2 * head_dim), F32)]),
        name="swa_attention",
    )(sinks, q, k2, v2)


def _tile(n, target):
    if n <= target:
        return n
    best = None
    for c in range(128, target + 1, 128):
        if n % c == 0:
            best = c
    assert best is not None, (n, target)
    return best


def kernel(x, norm1_g, norm2_g, ffn_w_in, ffn_w_out, lru_w_in, lru_conv_w, lru_conv_b, lru_w_rg,
           lru_b_rg, lru_w_ig, lru_b_ig, lru_lambda, lru_w_out, kv_norm_g, w_kv, k_norm_g, w_q,
           q_norm_g, sinks, w_o):
    batch, seq, d = x.shape
    t = batch * seq
    depth = norm1_g.shape[0]
    n_rec = lru_w_in.shape[0]
    head_dim = k_norm_g.shape[0]
    n_kv = w_kv.shape[1] // (2 * head_dim)
    n_heads = w_q.shape[2] // head_dim
    group = n_heads // n_kv
    assert seq % WINDOW == 0 and group % 2 == 0 and 2 * head_dim == 128
    assert MXU_COLS % head_dim == 0 and n_kv % 2 == 0

    tm = _tile(t, 1024)
    f = ffn_w_out.shape[1]
    tf = _tile(f, 512)
    tc = _tile(seq, 256)

    row = lambda v: v.reshape(1, -1).astype(F32)
    gidx = jnp.arange(MXU_COLS) // head_dim
    gsum = (gidx[:, None] == gidx[None, :]).astype(BF16)

    lru_w_in16 = lru_w_in.astype(BF16)
    lru_w_rg16, lru_w_ig16 = lru_w_rg.astype(BF16), lru_w_ig.astype(BF16)
    chain = _ffn_cast_ok(t, d, f, tm, tf)
    late = {"lru_w_out": lru_w_out, "w_q": w_q, "w_o": w_o}
    if chain:
        late.update(ffn_w_in=ffn_w_in, ffn_w_out=ffn_w_out)
    in_scan = {k: v for k, v in late.items()
               if n_rec > 0 and _cast_split(v.shape[1], v.shape[2], batch, seq // tc)}
    w16 = {k: v.astype(BF16) for k, v in late.items() if k not in in_scan and not k.startswith("ffn")}
    if not chain:
        w16.update(ffn_w_in=ffn_w_in.astype(BF16), ffn_w_out=ffn_w_out.astype(BF16))
    elif "ffn_w_in" not in in_scan or "ffn_w_out" not in in_scan:
        in_scan.pop("ffn_w_in", None), in_scan.pop("ffn_w_out", None)
        w16.update(ffn_w_in=ffn_w_in[:1].astype(BF16), ffn_w_out=ffn_w_out[:1].astype(BF16))

    xs = x.reshape(t, d)
    k2 = v2 = None
    for layer in range(depth):
        if layer < n_rec:
            i = layer
            gy, xbr = _rec_in_proj(xs, row(norm1_g[layer]), lru_w_in16, i, tm=tm,
                                   tn=lru_w_in.shape[2] // 2)
            casts = [(v, 0, 1 if k.startswith("ffn") else v.shape[0])
                     for k, v in in_scan.items()] if layer == 0 else []
            outs = _rglru(xbr, gy, lru_conv_w[i], row(lru_conv_b[i]), lru_w_rg16, row(lru_b_rg[i]),
                          lru_w_ig16, row(lru_b_ig[i]), row(lru_lambda[i]), i, casts,
                          batch=batch, tc=tc, lane_chunk=_tile(d, 512))
            m = outs[0]
            if casts:
                w16.update(zip(in_scan, outs[1:]))
            xs = _proj_residual(m, w16["lru_w_out"], i, xs, tm=_tile(t, 512), tn=d)
        else:
            if layer == n_rec:
                wk, wv = jnp.split(w_kv, 2, axis=-1)
                k2, v2 = _kv_proj(
                    xs, row(kv_norm_g), wk.astype(BF16), wv.astype(BF16),
                    row(jnp.tile(k_norm_g, n_kv)), gsum, head_dim=head_dim, tm=_tile(t, 512))
            j = layer - n_rec
            q = _q_proj(xs, row(norm1_g[layer]), w16["w_q"], j, row(jnp.tile(q_norm_g[j], n_heads)),
                        gsum, head_dim=head_dim, tm=tm, tn=n_heads * head_dim)
            o = _attention(sinks[j].astype(F32), q, k2, v2, batch=batch, n_kv=n_kv, group=group,
                           head_dim=head_dim, q_blocks=min(8, seq // WINDOW))
            xs = _proj_residual(o, w16["w_o"], j, xs, tm=_tile(t, 512), tn=d)
        if chain:
            nxt = (ffn_w_in, ffn_w_out, layer + 1) if layer + 1 < depth else None
            outs = _ffn(xs, row(norm2_g[layer]), w16["ffn_w_in"], w16["ffn_w_out"], 0, nxt,
                        tm=tm, tf=tf)
            xs = outs[0]
            if nxt is not None:
                w16.update(ffn_w_in=outs[1], ffn_w_out=outs[2])
        else:
            xs = _ffn(xs, row(norm2_g[layer]), w16["ffn_w_in"], w16["ffn_w_out"], layer,
                      tm=tm, tf=tf)[0]
    return xs.reshape(batch, seq, d)
```

```python
import functools
import math

import jax
import jax.numpy as jnp
from jax import lax
from jax.experimental import pallas as pl
from jax.experimental.pallas import tpu as pltpu

NORM_EPS = 1e-6
LRU_C = 8.0
WINDOW = 128
NEG_INF = -1e30
TINY_F32 = 1e-37
LOG2_E = 1.4426950408889634
MXU_COLS = 256
VMEM_CAP_BYTES = 58 * 1024 * 1024
COMPILER_TEMP_BYTES = 8 * 1024 * 1024

F32 = jnp.float32
BF16 = jnp.bfloat16


def _nbytes(shape, dtype):
    return math.prod(shape) * jnp.dtype(dtype).itemsize


def _params(sem, windows, scratch=()):
    need = sum(2 * _nbytes(*w) for w in windows) + sum(_nbytes(*b) for b in scratch)
    limit = min(need + COMPILER_TEMP_BYTES, VMEM_CAP_BYTES)
    return pltpu.CompilerParams(dimension_semantics=sem, vmem_limit_bytes=limit)


def _rms_rows(x, g):
    ms = jnp.mean(x * x, axis=-1, keepdims=True)
    return (x * lax.rsqrt(ms + NORM_EPS)) * g


def _gelu_tanh(x):
    c = math.sqrt(2.0 / math.pi)
    return 0.5 * x * (1.0 + jnp.tanh(c * (x + 0.044715 * (x * x * x))))


def _group_rms(q, gsum_ref, group):
    cols = q.shape[-1]
    chunk = min(cols, MXU_COLS)
    outs = []
    for c in range(cols // chunk):
        qc = q[:, c * chunk:(c + 1) * chunk]
        ssq = jnp.dot((qc * qc).astype(BF16), gsum_ref[:chunk, :chunk], preferred_element_type=F32)
        outs.append(qc * lax.rsqrt(ssq * (1.0 / group) + NORM_EPS))
    return outs[0] if len(outs) == 1 else jnp.concatenate(outs, axis=-1)


def _repeat_heads(a, head_dim):
    pair_w = 2 * head_dim
    lane = lax.broadcasted_iota(jnp.int32, (a.shape[0], pair_w), 1)
    lo = lane < head_dim
    outs = []
    for c in range(a.shape[1] // pair_w):
        slab = a[:, c * pair_w:(c + 1) * pair_w]
        swapped = pltpu.roll(slab, head_dim, axis=1)
        outs += [jnp.where(lo, slab, swapped), jnp.where(lo, swapped, slab)]
    return jnp.concatenate(outs, axis=1)


def _rec_in_kernel(x_ref, g_ref, wy_ref, wx_ref, oy_ref, ox_ref, *scratch, one_step):
    if one_step:
        h = _rms_rows(x_ref[...], g_ref[...]).astype(BF16)
    else:
        (h_ref,) = scratch

        @pl.when(pl.program_id(1) == 0)
        def _():
            h_ref[...] = _rms_rows(x_ref[...], g_ref[...]).astype(BF16)

        h = h_ref[...]
    y = jnp.dot(h, wy_ref[...], preferred_element_type=F32)
    oy_ref[...] = _gelu_tanh(y).astype(BF16)
    ox_ref[...] = jnp.dot(h, wx_ref[...], preferred_element_type=F32).astype(BF16)


def _rec_in_proj(x, g, w_in, layer, *, tm, tn):
    t, d = x.shape
    w = w_in.shape[2] // 2
    nj = w // tn
    h_scratch = [] if nj == 1 else [((tm, d), BF16)]
    return pl.pallas_call(
        functools.partial(_rec_in_kernel, one_step=nj == 1),
        grid=(t // tm, nj),
        in_specs=[
            pl.BlockSpec((tm, d), lambda i, j: (i, 0)),
            pl.BlockSpec((1, d), lambda i, j: (0, 0)),
            pl.BlockSpec((None, d, tn), lambda i, j: (layer, 0, j)),
            pl.BlockSpec((None, d, tn), lambda i, j: (layer, 0, j + nj)),
        ],
        out_specs=[
            pl.BlockSpec((tm, tn), lambda i, j: (i, j)),
            pl.BlockSpec((tm, tn), lambda i, j: (i, j)),
        ],
        out_shape=[jax.ShapeDtypeStruct((t, w), BF16), jax.ShapeDtypeStruct((t, w), BF16)],
        scratch_shapes=[pltpu.VMEM(*b) for b in h_scratch],
        compiler_params=_params(
            ("parallel", "arbitrary"),
            [((tm, d), F32), ((d, tn), BF16), ((d, tn), BF16), ((tm, tn), BF16), ((tm, tn), BF16)],
            h_scratch),
        name="rec_in_proj",
    )(x, g, w_in, w_in)


def _rglru_kernel(xbr_ref, gy_ref, perm_ref, permt_ref, cw_ref, cb_ref, wrg_ref, brg_ref, wig_ref,
                  big_ref, lam_ref, *rest, lane_chunk, n_cast):
    o_ref = rest[n_cast]
    for src_ref, dst_ref in zip(rest[:n_cast], rest[n_cast + 1:2 * n_cast + 1]):
        dst_ref[...] = src_ref[...].astype(BF16)
    tail_ref, carry_ref, a_ref, u_ref, gp_ref, m_ref = rest[2 * n_cast + 1:]
    tc, d = xbr_ref.shape
    taps = cw_ref.shape[0]
    seg = tc // 8
    halo = taps - 1

    @pl.when(pl.program_id(1) == 0)
    def _():
        tail_ref[...] = jnp.zeros_like(tail_ref)
        carry_ref[...] = jnp.zeros_like(carry_ref)

    perm = perm_ref[...]
    x0 = jnp.dot(perm, xbr_ref[...], preferred_element_type=F32)
    gp_ref[...] = jnp.dot(perm, gy_ref[...], preferred_element_type=F32)

    first_seg = lax.broadcasted_iota(jnp.int32, (8, d), 0) == 0
    tail = tail_ref[...]
    wrapped = []
    for i in range(halo):
        cur = x0[(seg - halo + i) * 8:(seg - halo + i + 1) * 8]
        prv = tail[i * 8:(i + 1) * 8]
        wrapped.append(jnp.where(first_seg, pltpu.roll(prv, 1, axis=0), pltpu.roll(cur, 1, axis=0)))
    tail_ref[...] = x0[(seg - halo) * 8:]

    cw_half = 0.5 * cw_ref[...]
    xh = x0 * cw_half[taps - 1:taps, :] + 0.5 * cb_ref[...]
    for k in range(1, taps):
        xk = jnp.concatenate(wrapped[halo - k:] + [x0[:tc - 8 * k]], axis=0)
        xh = xh + xk * cw_half[taps - 1 - k:taps - k, :]

    xh16 = xh.astype(BF16)
    nblk = wrg_ref.shape[0]
    bw = d // nblk
    for n in range(nblk):
        sl = slice(n * bw, (n + 1) * bw)
        xs = xh16[:, sl]
        tr = jnp.tanh(jnp.dot(xs, wrg_ref[n], preferred_element_type=F32) + 0.5 * brg_ref[:, sl])
        ti = jnp.tanh(jnp.dot(xs, wig_ref[n], preferred_element_type=F32) + 0.5 * big_ref[:, sl])
        nlam = -lam_ref[:, sl]
        softplus = jnp.maximum(nlam, 0.0) + jnp.log1p(jnp.exp(-jnp.abs(nlam)))
        half_c = (0.5 * LRU_C) * softplus
        z = tr * half_c + half_c
        a = jnp.exp2(z * (-LOG2_E))
        one_minus_a2 = jnp.tanh(z) * (1.0 + a * a)
        root = one_minus_a2 * lax.rsqrt(jnp.maximum(one_minus_a2, TINY_F32))
        a_ref[:, sl] = a
        u_ref[:, sl] = (root * xh[:, sl]) * (ti + 1.0)

    rowc = lax.broadcasted_iota(jnp.int32, (8, lane_chunk), 0)
    for c in range(d // lane_chunk):
        ls = slice(c * lane_chunk, (c + 1) * lane_chunk)

        h_end = u_ref[0:8, ls]
        e_end = a_ref[0:8, ls]
        for j in range(1, seg):
            av = a_ref[j * 8:(j + 1) * 8, ls]
            h_end = av * h_end + u_ref[j * 8:(j + 1) * 8, ls]
            e_end = av * e_end
            u_ref[j * 8:(j + 1) * 8, ls] = h_end
            a_ref[j * 8:(j + 1) * 8, ls] = e_end

        for s in (1, 2, 4):
            keep = rowc >= s
            e_prev = jnp.where(keep, pltpu.roll(e_end, s, axis=0), 1.0)
            h_prev = jnp.where(keep, pltpu.roll(h_end, s, axis=0), 0.0)
            h_end = e_end * h_prev + h_end
            e_end = e_end * e_prev
        state_in = carry_ref[:, ls]
        after = e_end * state_in + h_end
        seg_in = jnp.where(rowc == 0, state_in, pltpu.roll(after, 1, axis=0))
        carry_ref[:, ls] = jnp.broadcast_to(after[7:8, :], after.shape)
        seg_in2 = jnp.concatenate([seg_in, seg_in], axis=0)

        for jj in range(seg // 2):
            rs = slice(jj * 16, (jj + 1) * 16)
            h = u_ref[rs, ls] + a_ref[rs, ls] * seg_in2
            m_ref[rs, ls] = (h * gp_ref[rs, ls]).astype(BF16)

    o_ref[...] = jnp.dot(permt_ref[...], m_ref[...], preferred_element_type=F32).astype(BF16)


def _cast_split(rows, cols, batch, nt):
    if rows % nt == 0 and (rows // nt) % 16 == 0 and cols % batch == 0 and (cols // batch) % 128 == 0:
        return (rows // nt, cols // batch), lambda b, i: (i, b)
    if rows % batch == 0 and (rows // batch) % 16 == 0 and cols % nt == 0 and (cols // nt) % 128 == 0:
        return (rows // batch, cols // nt), lambda b, i: (b, i)
    return None


def _rglru(xbr, gy, conv_w, conv_b, w_rg, b_rg, w_ig, b_ig, lam, layer, casts=(), *,
           batch, tc, lane_chunk):
    t, d = xbr.shape
    s = t // batch
    nt = s // tc
    _, nblk, bw, _ = w_rg.shape
    taps = conv_w.shape[0]
    assert tc % 16 == 0 and taps - 1 <= tc // 8
    rows = jnp.arange(tc)
    src = (rows % 8) * (tc // 8) + rows // 8
    perm = (src[:, None] == rows[None, :]).astype(BF16)
    row = lambda b, i: (b * nt + i, 0)
    fixed2 = lambda b, i: (0, 0)
    fixed4 = lambda b, i: (layer, 0, 0, 0)
    in_specs = [
        pl.BlockSpec((tc, d), row),
        pl.BlockSpec((tc, d), row),
        pl.BlockSpec((tc, tc), fixed2),
        pl.BlockSpec((tc, tc), fixed2),
        pl.BlockSpec(conv_w.shape, fixed2),
        pl.BlockSpec((1, d), fixed2),
        pl.BlockSpec((None, nblk, bw, bw), fixed4),
        pl.BlockSpec((1, d), fixed2),
        pl.BlockSpec((None, nblk, bw, bw), fixed4),
        pl.BlockSpec((1, d), fixed2),
        pl.BlockSpec((1, d), fixed2),
    ]
    out_specs = [pl.BlockSpec((tc, d), row)]
    out_shape = [jax.ShapeDtypeStruct((t, d), BF16)]
    args = [xbr, gy, perm, perm.T, conv_w, conv_b, w_rg, b_rg, w_ig, b_ig, lam]
    windows = [((tc, d), BF16)] * 3 + [((tc, tc), BF16)] * 2 + [((nblk, bw, bw), BF16)] * 2
    for arr, first, count in casts:
        blk, where = _cast_split(arr.shape[1], arr.shape[2], batch, nt)
        assert first % count == 0
        in_specs.append(pl.BlockSpec((count,) + blk, lambda b, i, w=where, l=first // count: (l,) + w(b, i)))
    for arr, first, count in casts:
        blk, where = _cast_split(arr.shape[1], arr.shape[2], batch, nt)
        out_specs.append(pl.BlockSpec((count,) + blk, lambda b, i, w=where: (0,) + w(b, i)))
        out_shape.append(jax.ShapeDtypeStruct((count,) + arr.shape[1:], BF16))
        args.append(arr)
        windows += [((count,) + blk, F32), ((count,) + blk, BF16)]
    return pl.pallas_call(
        functools.partial(_rglru_kernel, lane_chunk=lane_chunk, n_cast=len(casts)),
        grid=(batch, nt),
        in_specs=in_specs,
        out_specs=out_specs,
        out_shape=out_shape,
        scratch_shapes=[
            pltpu.VMEM(((taps - 1) * 8, d), F32),
            pltpu.VMEM((8, d), F32),
            pltpu.VMEM((tc, d), F32),
            pltpu.VMEM((tc, d), F32),
            pltpu.VMEM((tc, d), F32),
            pltpu.VMEM((tc, d), BF16),
        ],
        compiler_params=_params(("parallel", "arbitrary"), windows,
                                [((tc, d), F32)] * 3 + [((tc, d), BF16)]),
        name="rglru_scan",
    )(*args)


def _proj_res_kernel(m_ref, w_ref, x_ref, o_ref):
    o_ref[...] = x_ref[...] + jnp.dot(m_ref[...], w_ref[...], preferred_element_type=F32)


def _proj_residual(m, w, layer, x, *, tm, tn):
    t, k = m.shape
    n = w.shape[2]
    return pl.pallas_call(
        _proj_res_kernel,
        grid=(t // tm, n // tn),
        in_specs=[
            pl.BlockSpec((tm, k), lambda i, j: (i, 0)),
            pl.BlockSpec((None, k, tn), lambda i, j: (layer, 0, j)),
            pl.BlockSpec((tm, tn), lambda i, j: (i, j)),
        ],
        out_specs=pl.BlockSpec((tm, tn), lambda i, j: (i, j)),
        out_shape=jax.ShapeDtypeStruct((t, n), F32),
        compiler_params=_params(
            ("parallel", "arbitrary"),
            [((tm, k), BF16), ((k, tn), BF16), ((tm, tn), F32), ((tm, tn), F32)]),
        name="proj_residual",
    )(m, w, x)


def _ffn_kernel(x_ref, g_ref, wg_ref, wu_ref, wo_ref, *rest, cast_next):
    if cast_next:
        nwi_ref, nwo_ref, o_ref, cwi_ref, cwo_ref, h_ref = rest
        cwi_ref[...] = nwi_ref[...].astype(BF16)
        cwo_ref[...] = nwo_ref[...].astype(BF16)
    else:
        o_ref, h_ref = rest

    def hidden_chunk(h):
        gate = jnp.dot(h, wg_ref[...], preferred_element_type=F32)
        up = jnp.dot(h, wu_ref[...], preferred_element_type=F32)
        act = ((gate * jax.nn.sigmoid(gate)) * up).astype(BF16)
        return jnp.dot(act, wo_ref[...], preferred_element_type=F32)

    @pl.when(pl.program_id(1) == 0)
    def _():
        x = x_ref[...]
        h = _rms_rows(x, g_ref[...]).astype(BF16)
        h_ref[...] = h
        o_ref[...] = x + hidden_chunk(h)

    @pl.when(pl.program_id(1) > 0)
    def _():
        o_ref[...] += hidden_chunk(h_ref[...])


def _ffn_cast_ok(t, d, f, tm, tf):
    ni, nf = t // tm, f // tf
    return (d % ni == 0 and (d // ni) % 128 == 0 and (2 * f) % nf == 0
            and (2 * f // nf) % 128 == 0)


def _ffn(x, g, w_in, w_out, layer, next_w=None, *, tm, tf):
    t, d = x.shape
    f = w_out.shape[1]
    ni, nf = t // tm, f // tf
    in_specs = [
        pl.BlockSpec((tm, d), lambda i, j: (i, 0)),
        pl.BlockSpec((1, d), lambda i, j: (0, 0)),
        pl.BlockSpec((None, d, tf), lambda i, j: (layer, 0, j)),
        pl.BlockSpec((None, d, tf), lambda i, j: (layer, 0, j + nf)),
        pl.BlockSpec((None, tf, d), lambda i, j: (layer, j, 0)),
    ]
    out_specs = [pl.BlockSpec((tm, d), lambda i, j: (i, 0))]
    out_shape = [jax.ShapeDtypeStruct((t, d), F32)]
    args = [x, g, w_in, w_in, w_out]
    windows = [((tm, d), F32)] * 2 + [((d, tf), BF16)] * 3
    if next_w is not None:
        nwi, nwo, nl = next_w
        ri, ci, co = d // ni, 2 * f // nf, d // ni
        in_specs += [pl.BlockSpec((None, ri, ci), lambda i, j: (nl, i, j)),
                     pl.BlockSpec((None, tf, co), lambda i, j: (nl, j, i))]
        out_specs += [pl.BlockSpec((None, ri, ci), lambda i, j: (0, i, j)),
                      pl.BlockSpec((None, tf, co), lambda i, j: (0, j, i))]
        out_shape += [jax.ShapeDtypeStruct((1, d, 2 * f), BF16), jax.ShapeDtypeStruct((1, f, d), BF16)]
        args += [nwi, nwo]
        windows += [((ri, ci), F32), ((tf, co), F32), ((ri, ci), BF16), ((tf, co), BF16)]
    return pl.pallas_call(
        functools.partial(_ffn_kernel, cast_next=next_w is not None),
        grid=(ni, nf),
        in_specs=in_specs,
        out_specs=out_specs,
        out_shape=out_shape,
        scratch_shapes=[pltpu.VMEM((tm, d), BF16)],
        compiler_params=_params(("parallel", "arbitrary"), windows, [((tm, d), BF16)]),
        name="swiglu_ffn",
    )(*args)


def _kv_kernel(x_ref, g_ref, wk_ref, wv_ref, kg_ref, gsum_ref, k_ref, v_ref, *, head_dim):
    h = _rms_rows(x_ref[...], g_ref[...]).astype(BF16)
    k = jnp.dot(h, wk_ref[...], preferred_element_type=F32)
    kn = _group_rms(k, gsum_ref, head_dim) * kg_ref[...]
    k_ref[...] = _repeat_heads(kn, head_dim).astype(BF16)
    v = jnp.dot(h, wv_ref[...], preferred_element_type=F32)
    v_ref[...] = _repeat_heads(v, head_dim).astype(BF16)


def _kv_proj(x, g, wk, wv, kg, gsum, *, head_dim, tm):
    t, d = x.shape
    nk = wk.shape[1]
    n = 2 * nk
    fixed = lambda i: (0, 0)
    return pl.pallas_call(
        functools.partial(_kv_kernel, head_dim=head_dim),
        grid=(t // tm,),
        in_specs=[
            pl.BlockSpec((tm, d), lambda i: (i, 0)),
            pl.BlockSpec((1, d), fixed),
            pl.BlockSpec((d, nk), fixed),
            pl.BlockSpec((d, nk), fixed),
            pl.BlockSpec((1, nk), fixed),
            pl.BlockSpec((MXU_COLS, MXU_COLS), fixed),
        ],
        out_specs=[pl.BlockSpec((tm, n), lambda i: (i, 0)), pl.BlockSpec((tm, n), lambda i: (i, 0))],
        out_shape=[jax.ShapeDtypeStruct((t, n), BF16), jax.ShapeDtypeStruct((t, n), BF16)],
        compiler_params=_params(
            ("parallel",), [((tm, d), F32), ((d, nk), BF16), ((d, nk), BF16), ((tm, n), BF16), ((tm, n), BF16)]),
        name="kv_proj",
    )(x, g, wk, wv, kg, gsum)


def _q_kernel(x_ref, g_ref, w_ref, qg_ref, gsum_ref, o_ref, *scratch, head_dim, one_step):
    if one_step:
        h = _rms_rows(x_ref[...], g_ref[...]).astype(BF16)
    else:
        (h_ref,) = scratch

        @pl.when(pl.program_id(1) == 0)
        def _():
            h_ref[...] = _rms_rows(x_ref[...], g_ref[...]).astype(BF16)

        h = h_ref[...]
    q = jnp.dot(h, w_ref[...], preferred_element_type=F32)
    qn = _group_rms(q, gsum_ref, head_dim) * qg_ref[...]
    o_ref[...] = (qn * (1.0 / math.sqrt(head_dim))).astype(BF16)


def _q_proj(x, g, w, layer, qg_t, gsum, *, head_dim, tm, tn):
    t, d = x.shape
    n = w.shape[2]
    h_scratch = [] if n == tn else [((tm, d), BF16)]
    return pl.pallas_call(
        functools.partial(_q_kernel, head_dim=head_dim, one_step=n == tn),
        grid=(t // tm, n // tn),
        in_specs=[
            pl.BlockSpec((tm, d), lambda i, j: (i, 0)),
            pl.BlockSpec((1, d), lambda i, j: (0, 0)),
            pl.BlockSpec((None, d, tn), lambda i, j: (layer, 0, j)),
            pl.BlockSpec((1, tn), lambda i, j: (0, j)),
            pl.BlockSpec((MXU_COLS, MXU_COLS), lambda i, j: (0, 0)),
        ],
        out_specs=pl.BlockSpec((tm, tn), lambda i, j: (i, j)),
        out_shape=jax.ShapeDtypeStruct((t, n), BF16),
        scratch_shapes=[pltpu.VMEM(*b) for b in h_scratch],
        compiler_params=_params(
            ("parallel", "arbitrary"), [((tm, d), F32), ((d, tn), BF16), ((tm, tn), BF16)],
            h_scratch),
        name="q_proj",
    )(x, g, w, qg_t, gsum)


def _attn_kernel(sink_ref, q_ref, k_ref, v_ref, o_ref, p_ref, es_ref, *,
                 n_kv, group, head_dim, q_blocks):
    blk = WINDOW
    inflight = p_ref.shape[0]
    pairs = group // 2
    pair_w = 2 * head_dim
    lane = lax.broadcasted_iota(jnp.int32, (blk, pair_w), 1)
    lo_half = lane < head_dim
    row = lax.broadcasted_iota(jnp.int32, (blk, blk), 0)
    col = lax.broadcasted_iota(jnp.int32, (blk, blk), 1)
    from_cur = col <= row
    cur16 = jnp.where(from_cur, 1.0, 0.0).astype(BF16)
    prev16 = jnp.where(from_cur, 0.0, 1.0).astype(BF16)
    lo_half32 = lane.astype(F32) < float(head_dim)
    zero = jnp.zeros((), BF16)

    def halves(prev, cur):
        return jnp.concatenate([
            jnp.where(lo_half, prev, zero), jnp.where(lo_half, cur, zero),
            jnp.where(lo_half, zero, prev), jnp.where(lo_half, zero, cur)], axis=0)

    lo16 = jnp.where(lo_half32, 1.0, 0.0).astype(BF16)
    hi16 = jnp.where(lo_half32, 0.0, 1.0).astype(BF16)
    ones_cat = jnp.concatenate([lo16, lo16, hi16, hi16], axis=0)

    def phase1(sub, n, r_q):
        r_cur = pl.multiple_of(n * blk, blk)
        r_prev = pl.multiple_of(jnp.maximum(n - 1, 0) * blk, blk)
        prev_bias = jnp.where(n > 0, 0.0, NEG_INF)
        for h in range(n_kv):
            hs = slice(h * pair_w, (h + 1) * pair_w)
            kcat = halves(k_ref[pl.ds(r_prev, blk), hs], k_ref[pl.ds(r_cur, blk), hs])
            qstack = jnp.concatenate(
                [q_ref[pl.ds(r_q, blk), (h * pairs + p) * pair_w:(h * pairs + p + 1) * pair_w]
                 for p in range(pairs)], axis=0)
            s = lax.dot_general(qstack, kcat, (((1,), (1,)), ((), ())),
                                preferred_element_type=F32)
            for p in range(pairs):
                ms = []
                for e in range(2):
                    s_prev = s[p * blk:(p + 1) * blk, (2 * e) * blk:(2 * e + 1) * blk] + prev_bias
                    s_cur = s[p * blk:(p + 1) * blk, (2 * e + 1) * blk:(2 * e + 2) * blk]
                    sc = jnp.where(from_cur, s_cur, s_prev)
                    m = jnp.max(sc, axis=-1, keepdims=True)
                    pexp = jnp.exp(sc - m).astype(BF16)
                    p_ref[sub, h, p * blk:(p + 1) * blk, (2 * e) * blk:(2 * e + 1) * blk] = (
                        pexp * prev16)
                    p_ref[sub, h, p * blk:(p + 1) * blk, (2 * e + 1) * blk:(2 * e + 2) * blk] = (
                        pexp * cur16)
                    ms.append(sink_ref[h * group + 2 * p + e] - m)
                es_ref[sub, h * pairs + p] = jnp.exp(jnp.where(lo_half32, ms[0], ms[1]))

    def phase3(sub, n, r_q):
        r_cur = pl.multiple_of(n * blk, blk)
        r_prev = pl.multiple_of(jnp.maximum(n - 1, 0) * blk, blk)
        for h in range(n_kv):
            hs = slice(h * pair_w, (h + 1) * pair_w)
            vcat = halves(v_ref[pl.ds(r_prev, blk), hs], v_ref[pl.ds(r_cur, blk), hs])
            pv = jnp.dot(p_ref[sub, h], jnp.concatenate([vcat, ones_cat], axis=1),
                         preferred_element_type=F32)
            for p in range(pairs):
                num = pv[p * blk:(p + 1) * blk, :pair_w]
                den = pv[p * blk:(p + 1) * blk, pair_w:] + es_ref[sub, h * pairs + p]
                o_ref[pl.ds(r_q, blk), (h * pairs + p) * pair_w:(h * pairs + p + 1) * pair_w] = (
                    num * (1.0 / den)).astype(BF16)

    def block_body(it, _):
        blocks = []
        for sub in range(inflight):
            qb = it * inflight + sub
            n = pl.program_id(1) * q_blocks + qb
            blocks.append((sub, n, pl.multiple_of(qb * blk, blk)))
        for sub, n, r_q in blocks:
            phase1(sub, n, r_q)
        for sub, n, r_q in blocks:
            phase3(sub, n, r_q)
        return 0

    lax.fori_loop(0, q_blocks // inflight, block_body, 0)


def _attention(sinks, q, k2, v2, *, batch, n_kv, group, head_dim, q_blocks):
    t, dq = q.shape
    s = t // batch
    nb = s // WINDOW
    inflight = 2 if q_blocks % 2 == 0 else 1
    assert nb % q_blocks == 0
    kw = k2.shape[1]
    n_heads = n_kv * group
    steps = nb // q_blocks
    qrow = lambda b, i: (b * steps + i, 0)
    whole = lambda b, i: (b, 0)
    return pl.pallas_call(
        functools.partial(_attn_kernel, n_kv=n_kv, group=group, head_dim=head_dim,
                          q_blocks=q_blocks),
        grid=(batch, steps),
        in_specs=[
            pl.BlockSpec(memory_space=pltpu.SMEM),
            pl.BlockSpec((q_blocks * WINDOW, dq), qrow),
            pl.BlockSpec((s, kw), whole),
            pl.BlockSpec((s, kw), whole),
        ],
        out_specs=pl.BlockSpec((q_blocks * WINDOW, dq), qrow),
        out_shape=jax.ShapeDtypeStruct((t, dq), BF16),
        scratch_shapes=[
            pltpu.VMEM((inflight, n_kv, (group // 2) * WINDOW, 4 * WINDOW), BF16),
            pltpu.VMEM((inflight, n_heads // 2, WINDOW, 2 * head_dim), F32),
        ],
        compiler_params=_params(
            ("parallel", "arbitrary"),
            [((q_blocks * WINDOW, dq), BF16)] * 2 + [((s, kw), BF16)] * 2,
            [((inflight, n_heads, WINDOW, WINDOW), BF16),
             ((inflight, n_heads // 2, WINDOW, 2 * head_dim), F32)]),
        name="swa_attention",
    )(sinks, q, k2, v2)


def _tile(n, target):
    if n <= target:
        return n
    best = None
    for c in range(128, target + 1, 128):
        if n % c == 0:
            best = c
    assert best is not None, (n, target)
    return best


def kernel(x, norm1_g, norm2_g, ffn_w_in, ffn_w_out, lru_w_in, lru_conv_w, lru_conv_b, lru_w_rg,
           lru_b_rg, lru_w_ig, lru_b_ig, lru_lambda, lru_w_out, kv_norm_g, w_kv, k_norm_g, w_q,
           q_norm_g, sinks, w_o):
    batch, seq, d = x.shape
    t = batch * seq
    depth = norm1_g.shape[0]
    n_rec = lru_w_in.shape[0]
    head_dim = k_norm_g.shape[0]
    n_kv = w_kv.shape[1] // (2 * head_dim)
    n_heads = w_q.shape[2] // head_dim
    group = n_heads // n_kv
    assert seq % WINDOW == 0 and group % 2 == 0 and 2 * head_dim == 128
    assert MXU_COLS % head_dim == 0 and n_kv % 2 == 0

    tm = _tile(t, 1024)
    f = ffn_w_out.shape[1]
    tf = _tile(f, 512)
    tc = _tile(seq, 256)

    row = lambda v: v.reshape(1, -1).astype(F32)
    gidx = jnp.arange(MXU_COLS) // head_dim
    gsum = (gidx[:, None] == gidx[None, :]).astype(BF16)

    lru_w_in16 = lru_w_in.astype(BF16)
    lru_w_rg16, lru_w_ig16 = lru_w_rg.astype(BF16), lru_w_ig.astype(BF16)
    chain = _ffn_cast_ok(t, d, f, tm, tf)
    late = {"lru_w_out": lru_w_out, "w_q": w_q, "w_o": w_o}
    if chain:
        late.update(ffn_w_in=ffn_w_in, ffn_w_out=ffn_w_out)
    in_scan = {k: v for k, v in late.items()
               if n_rec > 0 and _cast_split(v.shape[1], v.shape[2], batch, seq // tc)}
    w16 = {k: v.astype(BF16) for k, v in late.items() if k not in in_scan and not k.startswith("ffn")}
    if not chain:
        w16.update(ffn_w_in=ffn_w_in.astype(BF16), ffn_w_out=ffn_w_out.astype(BF16))
    elif "ffn_w_in" not in in_scan or "ffn_w_out" not in in_scan:
        in_scan.pop("ffn_w_in", None), in_scan.pop("ffn_w_out", None)
        w16.update(ffn_w_in=ffn_w_in[:1].astype(BF16), ffn_w_out=ffn_w_out[:1].astype(BF16))

    xs = x.reshape(t, d)
    k2 = v2 = None
    for layer in range(depth):
        if layer < n_rec:
            i = layer
            gy, xbr = _rec_in_proj(xs, row(norm1_g[layer]), lru_w_in16, i, tm=tm,
                                   tn=lru_w_in.shape[2] // 2)
            casts = [(v, 0, 1 if k.startswith("ffn") else v.shape[0])
                     for k, v in in_scan.items()] if layer == 0 else []
            outs = _rglru(xbr, gy, lru_conv_w[i], row(lru_conv_b[i]), lru_w_rg16, row(lru_b_rg[i]),
                          lru_w_ig16, row(lru_b_ig[i]), row(lru_lambda[i]), i, casts,
                          batch=batch, tc=tc, lane_chunk=_tile(d, 512))
            m = outs[0]
            if casts:
                w16.update(zip(in_scan, outs[1:]))
            xs = _proj_residual(m, w16["lru_w_out"], i, xs, tm=_tile(t, 512), tn=d)
        else:
            if layer == n_rec:
                wk, wv = jnp.split(w_kv, 2, axis=-1)
                k2, v2 = _kv_proj(
                    xs, row(kv_norm_g), wk.astype(BF16), wv.astype(BF16),
                    row(jnp.tile(k_norm_g, n_kv)), gsum, head_dim=head_dim, tm=_tile(t, 512))
            j = layer - n_rec
            q = _q_proj(xs, row(norm1_g[layer]), w16["w_q"], j, row(jnp.tile(q_norm_g[j], n_heads)),
                        gsum, head_dim=head_dim, tm=tm, tn=n_heads * head_dim)
            o = _attention(sinks[j].astype(F32), q, k2, v2, batch=batch, n_kv=n_kv, group=group,
                           head_dim=head_dim, q_blocks=min(8, seq // WINDOW))
            xs = _proj_residual(o, w16["w_o"], j, xs, tm=_tile(t, 512), tn=d)
        if chain:
            nxt = (ffn_w_in, ffn_w_out, layer + 1) if layer + 1 < depth else None
            outs = _ffn(xs, row(norm2_g[layer]), w16["ffn_w_in"], w16["ffn_w_out"], 0, nxt,
                        tm=tm, tf=tf)
            xs = outs[0]
            if nxt is not None:
                w16.update(ffn_w_in=outs[1], ffn_w_out=outs[2])
        else:
            xs = _ffn(xs, row(norm2_g[layer]), w16["ffn_w_in"], w16["ffn_w_out"], layer,
                      tm=tm, tf=tf)[0]
    return xs.reshape(batch, seq, d)
```

```python
import functools
import math

import jax
import jax.numpy as jnp
from jax import lax
from jax.experimental import pallas as pl
from jax.experimental.pallas import tpu as pltpu

NORM_EPS = 1e-6
LRU_C = 8.0
WINDOW = 128
NEG_INF = -1e30
TINY_F32 = 1e-37
LOG2_E = 1.4426950408889634
MXU_COLS = 256
VMEM_CAP_BYTES = 58 * 1024 * 1024
COMPILER_TEMP_BYTES = 8 * 1024 * 1024

F32 = jnp.float32
BF16 = jnp.bfloat16


def _nbytes(shape, dtype):
    return math.prod(shape) * jnp.dtype(dtype).itemsize


def _params(sem, windows, scratch=()):
    need = sum(2 * _nbytes(*w) for w in windows) + sum(_nbytes(*b) for b in scratch)
    limit = min(need + COMPILER_TEMP_BYTES, VMEM_CAP_BYTES)
    return pltpu.CompilerParams(dimension_semantics=sem, vmem_limit_bytes=limit)


def _rms_rows(x, g):
    ms = jnp.mean(x * x, axis=-1, keepdims=True)
    return (x * lax.rsqrt(ms + NORM_EPS)) * g


def _gelu_tanh(x):
    c = math.sqrt(2.0 / math.pi)
    return 0.5 * x * (1.0 + jnp.tanh(c * (x + 0.044715 * (x * x * x))))


def _group_rms(q, gsum_ref, group):
    cols = q.shape[-1]
    chunk = min(cols, MXU_COLS)
    outs = []
    for c in range(cols // chunk):
        qc = q[:, c * chunk:(c + 1) * chunk]
        ssq = jnp.dot((qc * qc).astype(BF16), gsum_ref[:chunk, :chunk], preferred_element_type=F32)
        outs.append(qc * lax.rsqrt(ssq * (1.0 / group) + NORM_EPS))
    return outs[0] if len(outs) == 1 else jnp.concatenate(outs, axis=-1)


def _repeat_heads(a, head_dim):
    pair_w = 2 * head_dim
    lane = lax.broadcasted_iota(jnp.int32, (a.shape[0], pair_w), 1)
    lo = lane < head_dim
    outs = []
    for c in range(a.shape[1] // pair_w):
        slab = a[:, c * pair_w:(c + 1) * pair_w]
        swapped = pltpu.roll(slab, head_dim, axis=1)
        outs += [jnp.where(lo, slab, swapped), jnp.where(lo, swapped, slab)]
    return jnp.concatenate(outs, axis=1)


def _rec_in_kernel(x_ref, g_ref, wy_ref, wx_ref, oy_ref, ox_ref, *scratch, one_step):
    if one_step:
        h = _rms_rows(x_ref[...], g_ref[...]).astype(BF16)
    else:
        (h_ref,) = scratch

        @pl.when(pl.program_id(1) == 0)
        def _():
            h_ref[...] = _rms_rows(x_ref[...], g_ref[...]).astype(BF16)

        h = h_ref[...]
    y = jnp.dot(h, wy_ref[...], preferred_element_type=F32)
    oy_ref[...] = _gelu_tanh(y).astype(BF16)
    ox_ref[...] = jnp.dot(h, wx_ref[...], preferred_element_type=F32).astype(BF16)


def _rec_in_proj(x, g, w_in, layer, *, tm, tn):
    t, d = x.shape
    w = w_in.shape[2] // 2
    nj = w // tn
    h_scratch = [] if nj == 1 else [((tm, d), BF16)]
    return pl.pallas_call(
        functools.partial(_rec_in_kernel, one_step=nj == 1),
        grid=(t // tm, nj),
        in_specs=[
            pl.BlockSpec((tm, d), lambda i, j: (i, 0)),
            pl.BlockSpec((1, d), lambda i, j: (0, 0)),
            pl.BlockSpec((None, d, tn), lambda i, j: (layer, 0, j)),
            pl.BlockSpec((None, d, tn), lambda i, j: (layer, 0, j + nj)),
        ],
        out_specs=[
            pl.BlockSpec((tm, tn), lambda i, j: (i, j)),
            pl.BlockSpec((tm, tn), lambda i, j: (i, j)),
        ],
        out_shape=[jax.ShapeDtypeStruct((t, w), BF16), jax.ShapeDtypeStruct((t, w), BF16)],
        scratch_shapes=[pltpu.VMEM(*b) for b in h_scratch],
        compiler_params=_params(
            ("parallel", "arbitrary"),
            [((tm, d), F32), ((d, tn), BF16), ((d, tn), BF16), ((tm, tn), BF16), ((tm, tn), BF16)],
            h_scratch),
        name="rec_in_proj",
    )(x, g, w_in, w_in)


def _rglru_kernel(xbr_ref, gy_ref, perm_ref, permt_ref, cw_ref, cb_ref, wrg_ref, brg_ref, wig_ref,
                  big_ref, lam_ref, *rest, lane_chunk, n_cast):
    o_ref = rest[n_cast]
    for src_ref, dst_ref in zip(rest[:n_cast], rest[n_cast + 1:2 * n_cast + 1]):
        dst_ref[...] = src_ref[...].astype(BF16)
    tail_ref, carry_ref, a_ref, u_ref, gp_ref, m_ref = rest[2 * n_cast + 1:]
    tc, d = xbr_ref.shape
    taps = cw_ref.shape[0]
    seg = tc // 8
    halo = taps - 1

    @pl.when(pl.program_id(1) == 0)
    def _():
        tail_ref[...] = jnp.zeros_like(tail_ref)
        carry_ref[...] = jnp.zeros_like(carry_ref)

    perm = perm_ref[...]
    x0 = jnp.dot(perm, xbr_ref[...], preferred_element_type=F32)
    gp_ref[...] = jnp.dot(perm, gy_ref[...], preferred_element_type=F32)

    first_seg = lax.broadcasted_iota(jnp.int32, (8, d), 0) == 0
    tail = tail_ref[...]
    wrapped = []
    for i in range(halo):
        cur = x0[(seg - halo + i) * 8:(seg - halo + i + 1) * 8]
        prv = tail[i * 8:(i + 1) * 8]
        wrapped.append(jnp.where(first_seg, pltpu.roll(prv, 1, axis=0), pltpu.roll(cur, 1, axis=0)))
    tail_ref[...] = x0[(seg - halo) * 8:]

    cw_half = 0.5 * cw_ref[...]
    xh = x0 * cw_half[taps - 1:taps, :] + 0.5 * cb_ref[...]
    for k in range(1, taps):
        xk = jnp.concatenate(wrapped[halo - k:] + [x0[:tc - 8 * k]], axis=0)
        xh = xh + xk * cw_half[taps - 1 - k:taps - k, :]

    xh16 = xh.astype(BF16)
    nblk = wrg_ref.shape[0]
    bw = d // nblk
    for n in range(nblk):
        sl = slice(n * bw, (n + 1) * bw)
        xs = xh16[:, sl]
        tr = jnp.tanh(jnp.dot(xs, wrg_ref[n], preferred_element_type=F32) + 0.5 * brg_ref[:, sl])
        ti = jnp.tanh(jnp.dot(xs, wig_ref[n], preferred_element_type=F32) + 0.5 * big_ref[:, sl])
        nlam = -lam_ref[:, sl]
        softplus = jnp.maximum(nlam, 0.0) + jnp.log1p(jnp.exp(-jnp.abs(nlam)))
        half_c = (0.5 * LRU_C) * softplus
        z = tr * half_c + half_c
        a = jnp.exp2(z * (-LOG2_E))
        one_minus_a2 = jnp.tanh(z) * (1.0 + a * a)
        root = one_minus_a2 * lax.rsqrt(jnp.maximum(one_minus_a2, TINY_F32))
        a_ref[:, sl] = a
        u_ref[:, sl] = (root * xh[:, sl]) * (ti + 1.0)

    rowc = lax.broadcasted_iota(jnp.int32, (8, lane_chunk), 0)
    for c in range(d // lane_chunk):
        ls = slice(c * lane_chunk, (c + 1) * lane_chunk)

        h_end = u_ref[0:8, ls]
        e_end = a_ref[0:8, ls]
        for j in range(1, seg):
            av = a_ref[j * 8:(j + 1) * 8, ls]
            h_end = av * h_end + u_ref[j * 8:(j + 1) * 8, ls]
            e_end = av * e_end
            u_ref[j * 8:(j + 1) * 8, ls] = h_end
            a_ref[j * 8:(j + 1) * 8, ls] = e_end

        for s in (1, 2, 4):
            keep = rowc >= s
            e_prev = jnp.where(keep, pltpu.roll(e_end, s, axis=0), 1.0)
            h_prev = jnp.where(keep, pltpu.roll(h_end, s, axis=0), 0.0)
            h_end = e_end * h_prev + h_end
            e_end = e_end * e_prev
        state_in = carry_ref[:, ls]
        after = e_end * state_in + h_end
        seg_in = jnp.where(rowc == 0, state_in, pltpu.roll(after, 1, axis=0))
        carry_ref[:, ls] = jnp.broadcast_to(after[7:8, :], after.shape)
        seg_in2 = jnp.concatenate([seg_in, seg_in], axis=0)

        for jj in range(seg // 2):
            rs = slice(jj * 16, (jj + 1) * 16)
            h = u_ref[rs, ls] + a_ref[rs, ls] * seg_in2
            m_ref[rs, ls] = (h * gp_ref[rs, ls]).astype(BF16)

    o_ref[...] = jnp.dot(permt_ref[...], m_ref[...], preferred_element_type=F32).astype(BF16)


def _cast_split(rows, cols, batch, nt):
    if rows % nt == 0 and (rows // nt) % 16 == 0 and cols % batch == 0 and (cols // batch) % 128 == 0:
        return (rows // nt, cols // batch), lambda b, i: (i, b)
    if rows % batch == 0 and (rows // batch) % 16 == 0 and cols % nt == 0 and (cols // nt) % 128 == 0:
        return (rows // batch, cols // nt), lambda b, i: (b, i)
    return None


def _rglru(xbr, gy, conv_w, conv_b, w_rg, b_rg, w_ig, b_ig, lam, layer, casts=(), *,
           batch, tc, lane_chunk):
    t, d = xbr.shape
    s = t // batch
    nt = s // tc
    _, nblk, bw, _ = w_rg.shape
    taps = conv_w.shape[0]
    assert tc % 16 == 0 and taps - 1 <= tc // 8
    rows = jnp.arange(tc)
    src = (rows % 8) * (tc // 8) + rows // 8
    perm = (src[:, None] == rows[None, :]).astype(BF16)
    row = lambda b, i: (b * nt + i, 0)
    fixed2 = lambda b, i: (0, 0)
    fixed4 = lambda b, i: (layer, 0, 0, 0)
    in_specs = [
        pl.BlockSpec((tc, d), row),
        pl.BlockSpec((tc, d), row),
        pl.BlockSpec((tc, tc), fixed2),
        pl.BlockSpec((tc, tc), fixed2),
        pl.BlockSpec(conv_w.shape, fixed2),
        pl.BlockSpec((1, d), fixed2),
        pl.BlockSpec((None, nblk, bw, bw), fixed4),
        pl.BlockSpec((1, d), fixed2),
        pl.BlockSpec((None, nblk, bw, bw), fixed4),
        pl.BlockSpec((1, d), fixed2),
        pl.BlockSpec((1, d), fixed2),
    ]
    out_specs = [pl.BlockSpec((tc, d), row)]
    out_shape = [jax.ShapeDtypeStruct((t, d), BF16)]
    args = [xbr, gy, perm, perm.T, conv_w, conv_b, w_rg, b_rg, w_ig, b_ig, lam]
    windows = [((tc, d), BF16)] * 3 + [((tc, tc), BF16)] * 2 + [((nblk, bw, bw), BF16)] * 2
    for arr, first, count in casts:
        blk, where = _cast_split(arr.shape[1], arr.shape[2], batch, nt)
        assert first % count == 0
        in_specs.append(pl.BlockSpec((count,) + blk, lambda b, i, w=where, l=first // count: (l,) + w(b, i)))
    for arr, first, count in casts:
        blk, where = _cast_split(arr.shape[1], arr.shape[2], batch, nt)
        out_specs.append(pl.BlockSpec((count,) + blk, lambda b, i, w=where: (0,) + w(b, i)))
        out_shape.append(jax.ShapeDtypeStruct((count,) + arr.shape[1:], BF16))
        args.append(arr)
        windows += [((count,) + blk, F32), ((count,) + blk, BF16)]
    return pl.pallas_call(
        functools.partial(_rglru_kernel, lane_chunk=lane_chunk, n_cast=len(casts)),
        grid=(batch, nt),
        in_specs=in_specs,
        out_specs=out_specs,
        out_shape=out_shape,
        scratch_shapes=[
            pltpu.VMEM(((taps - 1) * 8, d), F32),
            pltpu.VMEM((8, d), F32),
            pltpu.VMEM((tc, d), F32),
            pltpu.VMEM((tc, d), F32),
            pltpu.VMEM((tc, d), F32),
            pltpu.VMEM((tc, d), BF16),
        ],
        compiler_params=_params(("parallel", "arbitrary"), windows,
                                [((tc, d), F32)] * 3 + [((tc, d), BF16)]),
        name="rglru_scan",
    )(*args)


def _proj_res_kernel(m_ref, w_ref, x_ref, o_ref):
    o_ref[...] = x_ref[...] + jnp.dot(m_ref[...], w_ref[...], preferred_element_type=F32)


def _proj_residual(m, w, layer, x, *, tm, tn):
    t, k = m.shape
    n = w.shape[2]
    return pl.pallas_call(
        _proj_res_kernel,
        grid=(t // tm, n // tn),
        in_specs=[
            pl.BlockSpec((tm, k), lambda i, j: (i, 0)),
            pl.BlockSpec((None, k, tn), lambda i, j: (layer, 0, j)),
            pl.BlockSpec((tm, tn), lambda i, j: (i, j)),
        ],
        out_specs=pl.BlockSpec((tm, tn), lambda i, j: (i, j)),
        out_shape=jax.ShapeDtypeStruct((t, n), F32),
        compiler_params=_params(
            ("parallel", "arbitrary"),
            [((tm, k), BF16), ((k, tn), BF16), ((tm, tn), F32), ((tm, tn), F32)]),
        name="proj_residual",
    )(m, w, x)


def _ffn_kernel(x_ref, g_ref, wg_ref, wu_ref, wo_ref, *rest, cast_next):
    if cast_next:
        nwi_ref, nwo_ref, o_ref, cwi_ref, cwo_ref, h_ref = rest
        cwi_ref[...] = nwi_ref[...].astype(BF16)
        cwo_ref[...] = nwo_ref[...].astype(BF16)
    else:
        o_ref, h_ref = rest

    def hidden_chunk(h):
        gate = jnp.dot(h, wg_ref[...], preferred_element_type=F32)
        up = jnp.dot(h, wu_ref[...], preferred_element_type=F32)
        act = ((gate * jax.nn.sigmoid(gate)) * up).astype(BF16)
        return jnp.dot(act, wo_ref[...], preferred_element_type=F32)

    @pl.when(pl.program_id(1) == 0)
    def _():
        x = x_ref[...]
        h = _rms_rows(x, g_ref[...]).astype(BF16)
        h_ref[...] = h
        o_ref[...] = x + hidden_chunk(h)

    @pl.when(pl.program_id(1) > 0)
    def _():
        o_ref[...] += hidden_chunk(h_ref[...])


def _ffn_cast_ok(t, d, f, tm, tf):
    ni, nf = t // tm, f // tf
    return (d % ni == 0 and (d // ni) % 128 == 0 and (2 * f) % nf == 0
            and (2 * f // nf) % 128 == 0)


def _ffn(x, g, w_in, w_out, layer, next_w=None, *, tm, tf):
    t, d = x.shape
    f = w_out.shape[1]
    ni, nf = t // tm, f // tf
    in_specs = [
        pl.BlockSpec((tm, d), lambda i, j: (i, 0)),
        pl.BlockSpec((1, d), lambda i, j: (0, 0)),
        pl.BlockSpec((None, d, tf), lambda i, j: (layer, 0, j)),
        pl.BlockSpec((None, d, tf), lambda i, j: (layer, 0, j + nf)),
        pl.BlockSpec((None, tf, d), lambda i, j: (layer, j, 0)),
    ]
    out_specs = [pl.BlockSpec((tm, d), lambda i, j: (i, 0))]
    out_shape = [jax.ShapeDtypeStruct((t, d), F32)]
    args = [x, g, w_in, w_in, w_out]
    windows = [((tm, d), F32)] * 2 + [((d, tf), BF16)] * 3
    if next_w is not None:
        nwi, nwo, nl = next_w
        ri, ci, co = d // ni, 2 * f // nf, d // ni
        in_specs += [pl.BlockSpec((None, ri, ci), lambda i, j: (nl, i, j)),
                     pl.BlockSpec((None, tf, co), lambda i, j: (nl, j, i))]
        out_specs += [pl.BlockSpec((None, ri, ci), lambda i, j: (0, i, j)),
                      pl.BlockSpec((None, tf, co), lambda i, j: (0, j, i))]
        out_shape += [jax.ShapeDtypeStruct((1, d, 2 * f), BF16), jax.ShapeDtypeStruct((1, f, d), BF16)]
        args += [nwi, nwo]
        windows += [((ri, ci), F32), ((tf, co), F32), ((ri, ci), BF16), ((tf, co), BF16)]
    return pl.pallas_call(
        functools.partial(_ffn_kernel, cast_next=next_w is not None),
        grid=(ni, nf),
        in_specs=in_specs,
        out_specs=out_specs,
        out_shape=out_shape,
        scratch_shapes=[pltpu.VMEM((tm, d), BF16)],
        compiler_params=_params(("parallel", "arbitrary"), windows, [((tm, d), BF16)]),
        name="swiglu_ffn",
    )(*args)


def _q_kernel(x_ref, g_ref, w_ref, qg_ref, gsum_ref, *rest, head_dim, with_kv):
    x = x_ref[...]
    xn = x * lax.rsqrt(jnp.mean(x * x, axis=-1, keepdims=True) + NORM_EPS)
    if with_kv:
        gkv_ref, wk_ref, wv_ref, kg_ref, o_ref, k_ref, v_ref = rest
        hkv = (xn * gkv_ref[...]).astype(BF16)
        k = jnp.dot(hkv, wk_ref[...], preferred_element_type=F32)
        kn = _group_rms(k, gsum_ref, head_dim) * kg_ref[...]
        k_ref[...] = _repeat_heads(kn, head_dim).astype(BF16)
        v = jnp.dot(hkv, wv_ref[...], preferred_element_type=F32)
        v_ref[...] = _repeat_heads(v, head_dim).astype(BF16)
    else:
        (o_ref,) = rest
    q = jnp.dot((xn * g_ref[...]).astype(BF16), w_ref[...], preferred_element_type=F32)
    qn = _group_rms(q, gsum_ref, head_dim) * qg_ref[...]
    o_ref[...] = (qn * (1.0 / math.sqrt(head_dim))).astype(BF16)


def _q_proj(x, g, w, layer, qg_t, gsum, kv=None, *, head_dim, tm):
    t, d = x.shape
    n = w.shape[2]
    fixed = lambda i: (0, 0)
    in_specs = [
        pl.BlockSpec((tm, d), lambda i: (i, 0)),
        pl.BlockSpec((1, d), fixed),
        pl.BlockSpec((None, d, n), lambda i: (layer, 0, 0)),
        pl.BlockSpec((1, n), fixed),
        pl.BlockSpec((MXU_COLS, MXU_COLS), fixed),
    ]
    out_specs = [pl.BlockSpec((tm, n), lambda i: (i, 0))]
    out_shape = [jax.ShapeDtypeStruct((t, n), BF16)]
    args = [x, g, w, qg_t, gsum]
    windows = [((tm, d), F32), ((d, n), BF16), ((tm, n), BF16)]
    if kv is not None:
        nk = kv[1].shape[1]
        in_specs += [pl.BlockSpec((1, d), fixed), pl.BlockSpec((d, nk), fixed),
                     pl.BlockSpec((d, nk), fixed), pl.BlockSpec((1, nk), fixed)]
        out_specs += [pl.BlockSpec((tm, 2 * nk), lambda i: (i, 0))] * 2
        out_shape += [jax.ShapeDtypeStruct((t, 2 * nk), BF16)] * 2
        args += list(kv)
        windows += [((d, nk), BF16)] * 2 + [((tm, 2 * nk), BF16)] * 2
    return pl.pallas_call(
        functools.partial(_q_kernel, head_dim=head_dim, with_kv=kv is not None),
        grid=(t // tm,),
        in_specs=in_specs,
        out_specs=out_specs,
        out_shape=out_shape,
        compiler_params=_params(("parallel",), windows),
        name="q_proj",
    )(*args)


def _attn_kernel(sink_ref, q_ref, k_ref, v_ref, o_ref, p_ref, es_ref, *,
                 n_kv, group, head_dim, q_blocks):
    blk = WINDOW
    inflight = p_ref.shape[0]
    pairs = group // 2
    pair_w = 2 * head_dim
    lane = lax.broadcasted_iota(jnp.int32, (blk, pair_w), 1)
    lo_half = lane < head_dim
    row = lax.broadcasted_iota(jnp.int32, (blk, blk), 0)
    col = lax.broadcasted_iota(jnp.int32, (blk, blk), 1)
    from_cur = col <= row
    cur16 = jnp.where(from_cur, 1.0, 0.0).astype(BF16)
    prev16 = jnp.where(from_cur, 0.0, 1.0).astype(BF16)
    lo_half32 = lane.astype(F32) < float(head_dim)
    zero = jnp.zeros((), BF16)

    def halves(prev, cur):
        return jnp.concatenate([
            jnp.where(lo_half, prev, zero), jnp.where(lo_half, cur, zero),
            jnp.where(lo_half, zero, prev), jnp.where(lo_half, zero, cur)], axis=0)

    lo16 = jnp.where(lo_half32, 1.0, 0.0).astype(BF16)
    hi16 = jnp.where(lo_half32, 0.0, 1.0).astype(BF16)
    ones_cat = jnp.concatenate([lo16, lo16, hi16, hi16], axis=0)

    def phase1(sub, n, r_q):
        r_cur = pl.multiple_of(n * blk, blk)
        r_prev = pl.multiple_of(jnp.maximum(n - 1, 0) * blk, blk)
        prev_bias = jnp.where(n > 0, 0.0, NEG_INF)
        for h in range(n_kv):
            hs = slice(h * pair_w, (h + 1) * pair_w)
            kcat = halves(k_ref[pl.ds(r_prev, blk), hs], k_ref[pl.ds(r_cur, blk), hs])
            qstack = jnp.concatenate(
                [q_ref[pl.ds(r_q, blk), (h * pairs + p) * pair_w:(h * pairs + p + 1) * pair_w]
                 for p in range(pairs)], axis=0)
            s = lax.dot_general(qstack, kcat, (((1,), (1,)), ((), ())),
                                preferred_element_type=F32)
            for p in range(pairs):
                ms = []
                for e in range(2):
                    s_prev = s[p * blk:(p + 1) * blk, (2 * e) * blk:(2 * e + 1) * blk] + prev_bias
                    s_cur = s[p * blk:(p + 1) * blk, (2 * e + 1) * blk:(2 * e + 2) * blk]
                    sc = jnp.where(from_cur, s_cur, s_prev)
                    m = jnp.max(sc, axis=-1, keepdims=True)
                    pexp = jnp.exp(sc - m).astype(BF16)
                    p_ref[sub, h, p * blk:(p + 1) * blk, (2 * e) * blk:(2 * e + 1) * blk] = (
                        pexp * prev16)
                    p_ref[sub, h, p * blk:(p + 1) * blk, (2 * e + 1) * blk:(2 * e + 2) * blk] = (
                        pexp * cur16)
                    ms.append(sink_ref[h * group + 2 * p + e] - m)
                es_ref[sub, h * pairs + p] = jnp.exp(jnp.where(lo_half32, ms[0], ms[1]))

    def phase3(sub, n, r_q):
        r_cur = pl.multiple_of(n * blk, blk)
        r_prev = pl.multiple_of(jnp.maximum(n - 1, 0) * blk, blk)
        for h in range(n_kv):
            hs = slice(h * pair_w, (h + 1) * pair_w)
            vcat = halves(v_ref[pl.ds(r_prev, blk), hs], v_ref[pl.ds(r_cur, blk), hs])
            pv = jnp.dot(p_ref[sub, h], jnp.concatenate([vcat, ones_cat], axis=1),
                         preferred_element_type=F32)
            for p in range(pairs):
                num = pv[p * blk:(p + 1) * blk, :pair_w]
                den = pv[p * blk:(p + 1) * blk, pair_w:] + es_ref[sub, h * pairs + p]
                o_ref[pl.ds(r_q, blk), (h * pairs + p) * pair_w:(h * pairs + p + 1) * pair_w] = (
                    num * (1.0 / den)).astype(BF16)

    def block_body(it, _):
        blocks = []
        for sub in range(inflight):
            qb = it * inflight + sub
            n = pl.program_id(1) * q_blocks + qb
            blocks.append((sub, n, pl.multiple_of(qb * blk, blk)))
        for sub, n, r_q in blocks:
            phase1(sub, n, r_q)
        for sub, n, r_q in blocks:
            phase3(sub, n, r_q)
        return 0

    lax.fori_loop(0, q_blocks // inflight, block_body, 0)


def _attention(sinks, q, k2, v2, *, batch, n_kv, group, head_dim, q_blocks):
    t, dq = q.shape
    s = t // batch
    nb = s // WINDOW
    inflight = 2 if q_blocks % 2 == 0 else 1
    assert nb % q_blocks == 0
    kw = k2.shape[1]
    n_heads = n_kv * group
    steps = nb // q_blocks
    qrow = lambda b, i: (b * steps + i, 0)
    whole = lambda b, i: (b, 0)
    return pl.pallas_call(
        functools.partial(_attn_kernel, n_kv=n_kv, group=group, head_dim=head_dim,
                          q_blocks=q_blocks),
        grid=(batch, steps),
        in_specs=[
            pl.BlockSpec(memory_space=pltpu.SMEM),
            pl.BlockSpec((q_blocks * WINDOW, dq), qrow),
            pl.BlockSpec((s, kw), whole),
            pl.BlockSpec((s, kw), whole),
        ],
        out_specs=pl.BlockSpec((q_blocks * WINDOW, dq), qrow),
        out_shape=jax.ShapeDtypeStruct((t, dq), BF16),
        scratch_shapes=[
            pltpu.VMEM((inflight, n_kv, (group // 2) * WINDOW, 4 * WINDOW), BF16),
            pltpu.VMEM((inflight, n_heads // 2, WINDOW, 2 * head_dim), F32),
        ],
        compiler_params=_params(
            ("parallel", "arbitrary"),
            [((q_blocks * WINDOW, dq), BF16)] * 2 + [((s, kw), BF16)] * 2,
            [((inflight, n_heads, WINDOW, WINDOW), BF16),
             ((inflight, n_heads // 2, WINDOW, 2 * head_dim), F32)]),
        name="swa_attention",
    )(sinks, q, k2, v2)


def _tile(n, target):
    if n <= target:
        return n
    best = None
    for c in range(128, target + 1, 128):
        if n % c == 0:
            best = c
    assert best is not None, (n, target)
    return best


def kernel(x, norm1_g, norm2_g, ffn_w_in, ffn_w_out, lru_w_in, lru_conv_w, lru_conv_b, lru_w_rg,
           lru_b_rg, lru_w_ig, lru_b_ig, lru_lambda, lru_w_out, kv_norm_g, w_kv, k_norm_g, w_q,
           q_norm_g, sinks, w_o):
    batch, seq, d = x.shape
    t = batch * seq
    depth = norm1_g.shape[0]
    n_rec = lru_w_in.shape[0]
    head_dim = k_norm_g.shape[0]
    n_kv = w_kv.shape[1] // (2 * head_dim)
    n_heads = w_q.shape[2] // head_dim
    group = n_heads // n_kv
    assert seq % WINDOW == 0 and group % 2 == 0 and 2 * head_dim == 128
    assert MXU_COLS % head_dim == 0 and n_kv % 2 == 0

    tm = _tile(t, 1024)
    f = ffn_w_out.shape[1]
    tf = _tile(f, 512)
    tc = _tile(seq, 256)

    row = lambda v: v.reshape(1, -1).astype(F32)
    gidx = jnp.arange(MXU_COLS) // head_dim
    gsum = (gidx[:, None] == gidx[None, :]).astype(BF16)

    lru_w_in16 = lru_w_in.astype(BF16)
    lru_w_rg16, lru_w_ig16 = lru_w_rg.astype(BF16), lru_w_ig.astype(BF16)
    chain = _ffn_cast_ok(t, d, f, tm, tf)
    late = {"lru_w_out": lru_w_out, "w_q": w_q, "w_o": w_o}
    if chain:
        late.update(ffn_w_in=ffn_w_in, ffn_w_out=ffn_w_out)
    in_scan = {k: v for k, v in late.items()
               if n_rec > 0 and _cast_split(v.shape[1], v.shape[2], batch, seq // tc)}
    w16 = {k: v.astype(BF16) for k, v in late.items() if k not in in_scan and not k.startswith("ffn")}
    if not chain:
        w16.update(ffn_w_in=ffn_w_in.astype(BF16), ffn_w_out=ffn_w_out.astype(BF16))
    elif "ffn_w_in" not in in_scan or "ffn_w_out" not in in_scan:
        in_scan.pop("ffn_w_in", None), in_scan.pop("ffn_w_out", None)
        w16.update(ffn_w_in=ffn_w_in[:1].astype(BF16), ffn_w_out=ffn_w_out[:1].astype(BF16))

    xs = x.reshape(t, d)
    k2 = v2 = None
    for layer in range(depth):
        if layer < n_rec:
            i = layer
            gy, xbr = _rec_in_proj(xs, row(norm1_g[layer]), lru_w_in16, i, tm=tm,
                                   tn=lru_w_in.shape[2] // 2)
            casts = [(v, 0, 1 if k.startswith("ffn") else v.shape[0])
                     for k, v in in_scan.items()] if layer == 0 else []
            outs = _rglru(xbr, gy, lru_conv_w[i], row(lru_conv_b[i]), lru_w_rg16, row(lru_b_rg[i]),
                          lru_w_ig16, row(lru_b_ig[i]), row(lru_lambda[i]), i, casts,
                          batch=batch, tc=tc, lane_chunk=_tile(d, 512))
            m = outs[0]
            if casts:
                w16.update(zip(in_scan, outs[1:]))
            xs = _proj_residual(m, w16["lru_w_out"], i, xs, tm=_tile(t, 512), tn=d)
        else:
            j = layer - n_rec
            kv = None
            if layer == n_rec:
                wk, wv = jnp.split(w_kv, 2, axis=-1)
                kv = (row(kv_norm_g), wk.astype(BF16), wv.astype(BF16), row(jnp.tile(k_norm_g, n_kv)))
            outs = _q_proj(xs, row(norm1_g[layer]), w16["w_q"], j, row(jnp.tile(q_norm_g[j], n_heads)),
                           gsum, kv, head_dim=head_dim, tm=tm)
            q = outs[0]
            if kv is not None:
                k2, v2 = outs[1:]
            o = _attention(sinks[j].astype(F32), q, k2, v2, batch=batch, n_kv=n_kv, group=group,
                           head_dim=head_dim, q_blocks=min(8, seq // WINDOW))
            xs = _proj_residual(o, w16["w_o"], j, xs, tm=_tile(t, 512), tn=d)
        if chain:
            nxt = (ffn_w_in, ffn_w_out, layer + 1) if layer + 1 < depth else None
            outs = _ffn(xs, row(norm2_g[layer]), w16["ffn_w_in"], w16["ffn_w_out"], 0, nxt,
                        tm=tm, tf=tf)
            xs = outs[0]
            if nxt is not None:
                w16.update(ffn_w_in=outs[1], ffn_w_out=outs[2])
        else:
            xs = _ffn(xs, row(norm2_g[layer]), w16["ffn_w_in"], w16["ffn_w_out"], layer,
                      tm=tm, tf=tf)[0]
    return xs.reshape(batch, seq, d)
```

```python
import functools
import math

import jax
import jax.numpy as jnp
from jax import lax
from jax.experimental import pallas as pl
from jax.experimental.pallas import tpu as pltpu

NORM_EPS = 1e-6
LRU_C = 8.0
WINDOW = 128
NEG_INF = -1e30
TINY_F32 = 1e-37
LOG2_E = 1.4426950408889634
MXU_COLS = 256
VMEM_CAP_BYTES = 58 * 1024 * 1024
COMPILER_TEMP_BYTES = 12 * 1024 * 1024

F32 = jnp.float32
BF16 = jnp.bfloat16


def _nbytes(shape, dtype):
    return math.prod(shape) * jnp.dtype(dtype).itemsize


def _params(sem, windows, scratch=()):
    need = sum(2 * _nbytes(*w) for w in windows) + sum(_nbytes(*b) for b in scratch)
    limit = min(need + COMPILER_TEMP_BYTES, VMEM_CAP_BYTES)
    return pltpu.CompilerParams(dimension_semantics=sem, vmem_limit_bytes=limit)


def _rms_rows(x, g):
    ms = jnp.mean(x * x, axis=-1, keepdims=True)
    return (x * lax.rsqrt(ms + NORM_EPS)) * g


def _gelu_tanh(x):
    c = math.sqrt(2.0 / math.pi)
    return 0.5 * x * (1.0 + jnp.tanh(c * (x + 0.044715 * (x * x * x))))


def _group_rms(q, gsum_ref, group):
    cols = q.shape[-1]
    chunk = min(cols, MXU_COLS)
    outs = []
    for c in range(cols // chunk):
        qc = q[:, c * chunk:(c + 1) * chunk]
        ssq = jnp.dot((qc * qc).astype(BF16), gsum_ref[:chunk, :chunk], preferred_element_type=F32)
        outs.append(qc * lax.rsqrt(ssq * (1.0 / group) + NORM_EPS))
    return outs[0] if len(outs) == 1 else jnp.concatenate(outs, axis=-1)


def _repeat_heads(a, head_dim):
    pair_w = 2 * head_dim
    lane = lax.broadcasted_iota(jnp.int32, (a.shape[0], pair_w), 1)
    lo = lane < head_dim
    outs = []
    for c in range(a.shape[1] // pair_w):
        slab = a[:, c * pair_w:(c + 1) * pair_w]
        swapped = pltpu.roll(slab, head_dim, axis=1)
        outs += [jnp.where(lo, slab, swapped), jnp.where(lo, swapped, slab)]
    return jnp.concatenate(outs, axis=1)


def _rec_in_kernel(x_ref, g_ref, wy_ref, wx_ref, oy_ref, ox_ref, *scratch, one_step):
    if one_step:
        h = _rms_rows(x_ref[...], g_ref[...]).astype(BF16)
    else:
        (h_ref,) = scratch

        @pl.when(pl.program_id(1) == 0)
        def _():
            h_ref[...] = _rms_rows(x_ref[...], g_ref[...]).astype(BF16)

        h = h_ref[...]
    y = jnp.dot(h, wy_ref[...], preferred_element_type=F32)
    oy_ref[...] = _gelu_tanh(y).astype(BF16)
    ox_ref[...] = jnp.dot(h, wx_ref[...], preferred_element_type=F32).astype(BF16)


def _rec_in_proj(x, g, w_in, layer, *, tm, tn):
    t, d = x.shape
    w = w_in.shape[2] // 2
    nj = w // tn
    h_scratch = [] if nj == 1 else [((tm, d), BF16)]
    return pl.pallas_call(
        functools.partial(_rec_in_kernel, one_step=nj == 1),
        grid=(t // tm, nj),
        in_specs=[
            pl.BlockSpec((tm, d), lambda i, j: (i, 0)),
            pl.BlockSpec((1, d), lambda i, j: (0, 0)),
            pl.BlockSpec((None, d, tn), lambda i, j: (layer, 0, j)),
            pl.BlockSpec((None, d, tn), lambda i, j: (layer, 0, j + nj)),
        ],
        out_specs=[
            pl.BlockSpec((tm, tn), lambda i, j: (i, j)),
            pl.BlockSpec((tm, tn), lambda i, j: (i, j)),
        ],
        out_shape=[jax.ShapeDtypeStruct((t, w), BF16), jax.ShapeDtypeStruct((t, w), BF16)],
        scratch_shapes=[pltpu.VMEM(*b) for b in h_scratch],
        compiler_params=_params(
            ("parallel", "arbitrary"),
            [((tm, d), F32), ((d, tn), BF16), ((d, tn), BF16), ((tm, tn), BF16), ((tm, tn), BF16)],
            h_scratch),
        name="rec_in_proj",
    )(x, g, w_in, w_in)


def _rglru_kernel(xbr_ref, gy_ref, perm_ref, permt_ref, cw_ref, cb_ref, wrg_ref, brg_ref, wig_ref,
                  big_ref, lam_ref, *rest, lane_chunk, n_cast):
    o_ref = rest[n_cast]
    for src_ref, dst_ref in zip(rest[:n_cast], rest[n_cast + 1:2 * n_cast + 1]):
        dst_ref[...] = src_ref[...].astype(BF16)
    tail_ref, carry_ref, a_ref, u_ref, gp_ref, m_ref = rest[2 * n_cast + 1:]
    tc, d = xbr_ref.shape
    taps = cw_ref.shape[0]
    seg = tc // 8
    halo = taps - 1

    @pl.when(pl.program_id(1) == 0)
    def _():
        tail_ref[...] = jnp.zeros_like(tail_ref)
        carry_ref[...] = jnp.zeros_like(carry_ref)

    perm = perm_ref[...]
    x0 = jnp.dot(perm, xbr_ref[...], preferred_element_type=F32)
    gp_ref[...] = jnp.dot(perm, gy_ref[...], preferred_element_type=F32)

    first_seg = lax.broadcasted_iota(jnp.int32, (8, d), 0) == 0
    tail = tail_ref[...]
    wrapped = []
    for i in range(halo):
        cur = x0[(seg - halo + i) * 8:(seg - halo + i + 1) * 8]
        prv = tail[i * 8:(i + 1) * 8]
        wrapped.append(jnp.where(first_seg, pltpu.roll(prv, 1, axis=0), pltpu.roll(cur, 1, axis=0)))
    tail_ref[...] = x0[(seg - halo) * 8:]

    cw_half = 0.5 * cw_ref[...]
    xh = x0 * cw_half[taps - 1:taps, :] + 0.5 * cb_ref[...]
    for k in range(1, taps):
        xk = jnp.concatenate(wrapped[halo - k:] + [x0[:tc - 8 * k]], axis=0)
        xh = xh + xk * cw_half[taps - 1 - k:taps - k, :]

    xh16 = xh.astype(BF16)
    nblk = wrg_ref.shape[0]
    bw = d // nblk
    for n in range(nblk):
        sl = slice(n * bw, (n + 1) * bw)
        xs = xh16[:, sl]
        tr = jnp.tanh(jnp.dot(xs, wrg_ref[n], preferred_element_type=F32) + 0.5 * brg_ref[:, sl])
        ti = jnp.tanh(jnp.dot(xs, wig_ref[n], preferred_element_type=F32) + 0.5 * big_ref[:, sl])
        nlam = -lam_ref[:, sl]
        softplus = jnp.maximum(nlam, 0.0) + jnp.log1p(jnp.exp(-jnp.abs(nlam)))
        half_c = (0.5 * LRU_C) * softplus
        z = tr * half_c + half_c
        a = jnp.exp2(z * (-LOG2_E))
        one_minus_a2 = jnp.tanh(z) * (1.0 + a * a)
        root = one_minus_a2 * lax.rsqrt(jnp.maximum(one_minus_a2, TINY_F32))
        a_ref[:, sl] = a
        u_ref[:, sl] = (root * xh[:, sl]) * (ti + 1.0)

    rowc = lax.broadcasted_iota(jnp.int32, (8, lane_chunk), 0)
    for c in range(d // lane_chunk):
        ls = slice(c * lane_chunk, (c + 1) * lane_chunk)

        h_end = u_ref[0:8, ls]
        e_end = a_ref[0:8, ls]
        for j in range(1, seg):
            av = a_ref[j * 8:(j + 1) * 8, ls]
            h_end = av * h_end + u_ref[j * 8:(j + 1) * 8, ls]
            e_end = av * e_end
            u_ref[j * 8:(j + 1) * 8, ls] = h_end
            a_ref[j * 8:(j + 1) * 8, ls] = e_end

        for s in (1, 2, 4):
            keep = rowc >= s
            e_prev = jnp.where(keep, pltpu.roll(e_end, s, axis=0), 1.0)
            h_prev = jnp.where(keep, pltpu.roll(h_end, s, axis=0), 0.0)
            h_end = e_end * h_prev + h_end
            e_end = e_end * e_prev
        state_in = carry_ref[:, ls]
        after = e_end * state_in + h_end
        seg_in = jnp.where(rowc == 0, state_in, pltpu.roll(after, 1, axis=0))
        carry_ref[:, ls] = jnp.broadcast_to(after[7:8, :], after.shape)
        seg_in2 = jnp.concatenate([seg_in, seg_in], axis=0)

        for jj in range(seg // 2):
            rs = slice(jj * 16, (jj + 1) * 16)
            h = u_ref[rs, ls] + a_ref[rs, ls] * seg_in2
            m_ref[rs, ls] = (h * gp_ref[rs, ls]).astype(BF16)

    o_ref[...] = jnp.dot(permt_ref[...], m_ref[...], preferred_element_type=F32).astype(BF16)


def _cast_split(rows, cols, batch, nt):
    if rows % nt == 0 and (rows // nt) % 16 == 0 and cols % batch == 0 and (cols // batch) % 128 == 0:
        return (rows // nt, cols // batch), lambda b, i: (i, b)
    if rows % batch == 0 and (rows // batch) % 16 == 0 and cols % nt == 0 and (cols // nt) % 128 == 0:
        return (rows // batch, cols // nt), lambda b, i: (b, i)
    return None


def _rglru(xbr, gy, conv_w, conv_b, w_rg, b_rg, w_ig, b_ig, lam, layer, casts=(), *,
           batch, tc, lane_chunk):
    t, d = xbr.shape
    s = t // batch
    nt = s // tc
    _, nblk, bw, _ = w_rg.shape
    taps = conv_w.shape[0]
    assert tc % 16 == 0 and taps - 1 <= tc // 8
    rows = jnp.arange(tc)
    src = (rows % 8) * (tc // 8) + rows // 8
    perm = (src[:, None] == rows[None, :]).astype(BF16)
    row = lambda b, i: (b * nt + i, 0)
    fixed2 = lambda b, i: (0, 0)
    fixed4 = lambda b, i: (layer, 0, 0, 0)
    in_specs = [
        pl.BlockSpec((tc, d), row),
        pl.BlockSpec((tc, d), row),
        pl.BlockSpec((tc, tc), fixed2),
        pl.BlockSpec((tc, tc), fixed2),
        pl.BlockSpec(conv_w.shape, fixed2),
        pl.BlockSpec((1, d), fixed2),
        pl.BlockSpec((None, nblk, bw, bw), fixed4),
        pl.BlockSpec((1, d), fixed2),
        pl.BlockSpec((None, nblk, bw, bw), fixed4),
        pl.BlockSpec((1, d), fixed2),
        pl.BlockSpec((1, d), fixed2),
    ]
    out_specs = [pl.BlockSpec((tc, d), row)]
    out_shape = [jax.ShapeDtypeStruct((t, d), BF16)]
    args = [xbr, gy, perm, perm.T, conv_w, conv_b, w_rg, b_rg, w_ig, b_ig, lam]
    windows = [((tc, d), BF16)] * 3 + [((tc, tc), BF16)] * 2 + [((nblk, bw, bw), BF16)] * 2
    for arr, first, count in casts:
        blk, where = _cast_split(arr.shape[1], arr.shape[2], batch, nt)
        assert first % count == 0
        in_specs.append(pl.BlockSpec((count,) + blk, lambda b, i, w=where, l=first // count: (l,) + w(b, i)))
    for arr, first, count in casts:
        blk, where = _cast_split(arr.shape[1], arr.shape[2], batch, nt)
        out_specs.append(pl.BlockSpec((count,) + blk, lambda b, i, w=where: (0,) + w(b, i)))
        out_shape.append(jax.ShapeDtypeStruct((count,) + arr.shape[1:], BF16))
        args.append(arr)
        windows += [((count,) + blk, F32), ((count,) + blk, BF16)]
    return pl.pallas_call(
        functools.partial(_rglru_kernel, lane_chunk=lane_chunk, n_cast=len(casts)),
        grid=(batch, nt),
        in_specs=in_specs,
        out_specs=out_specs,
        out_shape=out_shape,
        scratch_shapes=[
            pltpu.VMEM(((taps - 1) * 8, d), F32),
            pltpu.VMEM((8, d), F32),
            pltpu.VMEM((tc, d), F32),
            pltpu.VMEM((tc, d), F32),
            pltpu.VMEM((tc, d), F32),
            pltpu.VMEM((tc, d), BF16),
        ],
        compiler_params=_params(("parallel", "arbitrary"), windows,
                                [((tc, d), F32)] * 3 + [((tc, d), BF16)]),
        name="rglru_scan",
    )(*args)


def _proj_res_kernel(m_ref, w_ref, x_ref, o_ref):
    o_ref[...] = x_ref[...] + jnp.dot(m_ref[...], w_ref[...], preferred_element_type=F32)


def _proj_residual(m, w, layer, x, *, tm, tn):
    t, k = m.shape
    n = w.shape[2]
    return pl.pallas_call(
        _proj_res_kernel,
        grid=(t // tm, n // tn),
        in_specs=[
            pl.BlockSpec((tm, k), lambda i, j: (i, 0)),
            pl.BlockSpec((None, k, tn), lambda i, j: (layer, 0, j)),
            pl.BlockSpec((tm, tn), lambda i, j: (i, j)),
        ],
        out_specs=pl.BlockSpec((tm, tn), lambda i, j: (i, j)),
        out_shape=jax.ShapeDtypeStruct((t, n), F32),
        compiler_params=_params(
            ("parallel", "arbitrary"),
            [((tm, k), BF16), ((k, tn), BF16), ((tm, tn), F32), ((tm, tn), F32)]),
        name="proj_residual",
    )(m, w, x)


def _ffn_kernel(x_ref, g_ref, wg_ref, wu_ref, wo_ref, *rest, cast_next):
    if cast_next:
        nwi_ref, nwo_ref, o_ref, cwi_ref, cwo_ref, h_ref = rest
        cwi_ref[...] = nwi_ref[...].astype(BF16)
        cwo_ref[...] = nwo_ref[...].astype(BF16)
    else:
        o_ref, h_ref = rest

    def hidden_chunk(h):
        gate = jnp.dot(h, wg_ref[...], preferred_element_type=F32)
        up = jnp.dot(h, wu_ref[...], preferred_element_type=F32)
        act = ((gate * jax.nn.sigmoid(gate)) * up).astype(BF16)
        return jnp.dot(act, wo_ref[...], preferred_element_type=F32)

    @pl.when(pl.program_id(1) == 0)
    def _():
        x = x_ref[...]
        h = _rms_rows(x, g_ref[...]).astype(BF16)
        h_ref[...] = h
        o_ref[...] = x + hidden_chunk(h)

    @pl.when(pl.program_id(1) > 0)
    def _():
        o_ref[...] += hidden_chunk(h_ref[...])


def _ffn_cast_ok(t, d, f, tm, tf):
    ni, nf = t // tm, f // tf
    return (d % ni == 0 and (d // ni) % 128 == 0 and (2 * f) % nf == 0
            and (2 * f // nf) % 128 == 0)


def _ffn(x, g, w_in, w_out, layer, next_w=None, *, tm, tf):
    t, d = x.shape
    f = w_out.shape[1]
    ni, nf = t // tm, f // tf
    in_specs = [
        pl.BlockSpec((tm, d), lambda i, j: (i, 0)),
        pl.BlockSpec((1, d), lambda i, j: (0, 0)),
        pl.BlockSpec((None, d, tf), lambda i, j: (layer, 0, j)),
        pl.BlockSpec((None, d, tf), lambda i, j: (layer, 0, j + nf)),
        pl.BlockSpec((None, tf, d), lambda i, j: (layer, j, 0)),
    ]
    out_specs = [pl.BlockSpec((tm, d), lambda i, j: (i, 0))]
    out_shape = [jax.ShapeDtypeStruct((t, d), F32)]
    args = [x, g, w_in, w_in, w_out]
    windows = [((tm, d), F32)] * 2 + [((d, tf), BF16)] * 3
    if next_w is not None:
        nwi, nwo, nl = next_w
        ri, ci, co = d // ni, 2 * f // nf, d // ni
        in_specs += [pl.BlockSpec((None, ri, ci), lambda i, j: (nl, i, j)),
                     pl.BlockSpec((None, tf, co), lambda i, j: (nl, j, i))]
        out_specs += [pl.BlockSpec((None, ri, ci), lambda i, j: (0, i, j)),
                      pl.BlockSpec((None, tf, co), lambda i, j: (0, j, i))]
        out_shape += [jax.ShapeDtypeStruct((1, d, 2 * f), BF16), jax.ShapeDtypeStruct((1, f, d), BF16)]
        args += [nwi, nwo]
        windows += [((ri, ci), F32), ((tf, co), F32), ((ri, ci), BF16), ((tf, co), BF16)]
    return pl.pallas_call(
        functools.partial(_ffn_kernel, cast_next=next_w is not None),
        grid=(ni, nf),
        in_specs=in_specs,
        out_specs=out_specs,
        out_shape=out_shape,
        scratch_shapes=[pltpu.VMEM((tm, d), BF16)],
        compiler_params=_params(("parallel", "arbitrary"), windows, [((tm, d), BF16)]),
        name="swiglu_ffn",
    )(*args)


def _q_kernel(x_ref, g_ref, w_ref, qg_ref, gsum_ref, *rest, head_dim, with_kv):
    x = x_ref[...]
    xn = x * lax.rsqrt(jnp.mean(x * x, axis=-1, keepdims=True) + NORM_EPS)
    if with_kv:
        gkv_ref, wk_ref, wv_ref, kg_ref, o_ref, k_ref, v_ref = rest
        hkv = (xn * gkv_ref[...]).astype(BF16)
        k = jnp.dot(hkv, wk_ref[...], preferred_element_type=F32)
        kn = _group_rms(k, gsum_ref, head_dim) * kg_ref[...]
        k_ref[...] = _repeat_heads(kn, head_dim).astype(BF16)
        v = jnp.dot(hkv, wv_ref[...], preferred_element_type=F32)
        v_ref[...] = _repeat_heads(v, head_dim).astype(BF16)
    else:
        (o_ref,) = rest
    q = jnp.dot((xn * g_ref[...]).astype(BF16), w_ref[...], preferred_element_type=F32)
    qn = _group_rms(q, gsum_ref, head_dim) * qg_ref[...]
    o_ref[...] = (qn * (1.0 / math.sqrt(head_dim))).astype(BF16)


def _q_proj(x, g, w, layer, qg_t, gsum, kv=None, *, head_dim, tm):
    t, d = x.shape
    n = w.shape[2]
    fixed = lambda i: (0, 0)
    in_specs = [
        pl.BlockSpec((tm, d), lambda i: (i, 0)),
        pl.BlockSpec((1, d), fixed),
        pl.BlockSpec((None, d, n), lambda i: (layer, 0, 0)),
        pl.BlockSpec((1, n), fixed),
        pl.BlockSpec((MXU_COLS, MXU_COLS), fixed),
    ]
    out_specs = [pl.BlockSpec((tm, n), lambda i: (i, 0))]
    out_shape = [jax.ShapeDtypeStruct((t, n), BF16)]
    args = [x, g, w, qg_t, gsum]
    windows = [((tm, d), F32), ((d, n), BF16), ((tm, n), BF16)]
    if kv is not None:
        nk = kv[1].shape[1]
        in_specs += [pl.BlockSpec((1, d), fixed), pl.BlockSpec((d, nk), fixed),
                     pl.BlockSpec((d, nk), fixed), pl.BlockSpec((1, nk), fixed)]
        out_specs += [pl.BlockSpec((tm, 2 * nk), lambda i: (i, 0))] * 2
        out_shape += [jax.ShapeDtypeStruct((t, 2 * nk), BF16)] * 2
        args += list(kv)
        windows += [((d, nk), BF16)] * 2 + [((tm, 2 * nk), BF16)] * 2
    return pl.pallas_call(
        functools.partial(_q_kernel, head_dim=head_dim, with_kv=kv is not None),
        grid=(t // tm,),
        in_specs=in_specs,
        out_specs=out_specs,
        out_shape=out_shape,
        compiler_params=_params(("parallel",), windows),
        name="q_proj",
    )(*args)


def _attn_kernel(sink_ref, q_ref, k_ref, v_ref, o_ref, p_ref, es_ref, *,
                 n_kv, group, head_dim, q_blocks):
    blk = WINDOW
    inflight = p_ref.shape[0]
    pairs = group // 2
    pair_w = 2 * head_dim
    lane = lax.broadcasted_iota(jnp.int32, (blk, pair_w), 1)
    lo_half = lane < head_dim
    row = lax.broadcasted_iota(jnp.int32, (blk, blk), 0)
    col = lax.broadcasted_iota(jnp.int32, (blk, blk), 1)
    from_cur = col <= row
    cur16 = jnp.where(from_cur, 1.0, 0.0).astype(BF16)
    prev16 = jnp.where(from_cur, 0.0, 1.0).astype(BF16)
    lo_half32 = lane.astype(F32) < float(head_dim)
    zero = jnp.zeros((), BF16)

    def halves(prev, cur):
        return jnp.concatenate([
            jnp.where(lo_half, prev, zero), jnp.where(lo_half, cur, zero),
            jnp.where(lo_half, zero, prev), jnp.where(lo_half, zero, cur)], axis=0)

    lo16 = jnp.where(lo_half32, 1.0, 0.0).astype(BF16)
    hi16 = jnp.where(lo_half32, 0.0, 1.0).astype(BF16)
    ones_cat = jnp.concatenate([lo16, lo16, hi16, hi16], axis=0)

    def phase1(sub, n, r_q):
        r_cur = pl.multiple_of(n * blk, blk)
        r_prev = pl.multiple_of(jnp.maximum(n - 1, 0) * blk, blk)
        prev_bias = jnp.where(n > 0, 0.0, NEG_INF)
        for h in range(n_kv):
            hs = slice(h * pair_w, (h + 1) * pair_w)
            kcat = halves(k_ref[pl.ds(r_prev, blk), hs], k_ref[pl.ds(r_cur, blk), hs])
            qstack = jnp.concatenate(
                [q_ref[pl.ds(r_q, blk), (h * pairs + p) * pair_w:(h * pairs + p + 1) * pair_w]
                 for p in range(pairs)], axis=0)
            s = lax.dot_general(qstack, kcat, (((1,), (1,)), ((), ())),
                                preferred_element_type=F32)
            for p in range(pairs):
                ms = []
                for e in range(2):
                    s_prev = s[p * blk:(p + 1) * blk, (2 * e) * blk:(2 * e + 1) * blk] + prev_bias
                    s_cur = s[p * blk:(p + 1) * blk, (2 * e + 1) * blk:(2 * e + 2) * blk]
                    sc = jnp.where(from_cur, s_cur, s_prev)
                    m = jnp.max(sc, axis=-1, keepdims=True)
                    pexp = jnp.exp(sc - m).astype(BF16)
                    p_ref[sub, h, p * blk:(p + 1) * blk, (2 * e) * blk:(2 * e + 1) * blk] = (
                        pexp * prev16)
                    p_ref[sub, h, p * blk:(p + 1) * blk, (2 * e + 1) * blk:(2 * e + 2) * blk] = (
                        pexp * cur16)
                    ms.append(sink_ref[h * group + 2 * p + e] - m)
                es_ref[sub, h * pairs + p] = jnp.exp(jnp.where(lo_half32, ms[0], ms[1]))

    def phase3(sub, n, r_q):
        r_cur = pl.multiple_of(n * blk, blk)
        r_prev = pl.multiple_of(jnp.maximum(n - 1, 0) * blk, blk)
        for h in range(n_kv):
            hs = slice(h * pair_w, (h + 1) * pair_w)
            vcat = halves(v_ref[pl.ds(r_prev, blk), hs], v_ref[pl.ds(r_cur, blk), hs])
            pv = jnp.dot(p_ref[sub, h], jnp.concatenate([vcat, ones_cat], axis=1),
                         preferred_element_type=F32)
            for p in range(pairs):
                num = pv[p * blk:(p + 1) * blk, :pair_w]
                den = pv[p * blk:(p + 1) * blk, pair_w:] + es_ref[sub, h * pairs + p]
                o_ref[pl.ds(r_q, blk), (h * pairs + p) * pair_w:(h * pairs + p + 1) * pair_w] = (
                    num * (1.0 / den)).astype(BF16)

    def block_body(it, _):
        blocks = []
        for sub in range(inflight):
            qb = it * inflight + sub
            n = pl.program_id(1) * q_blocks + qb
            blocks.append((sub, n, pl.multiple_of(qb * blk, blk)))
        for sub, n, r_q in blocks:
            phase1(sub, n, r_q)
        for sub, n, r_q in blocks:
            phase3(sub, n, r_q)
        return 0

    lax.fori_loop(0, q_blocks // inflight, block_body, 0)


def _attention(sinks, q, k2, v2, *, batch, n_kv, group, head_dim, q_blocks):
    t, dq = q.shape
    s = t // batch
    nb = s // WINDOW
    inflight = 4 if q_blocks % 4 == 0 else 1
    assert nb % q_blocks == 0
    kw = k2.shape[1]
    n_heads = n_kv * group
    steps = nb // q_blocks
    qrow = lambda b, i: (b * steps + i, 0)
    whole = lambda b, i: (b, 0)
    return pl.pallas_call(
        functools.partial(_attn_kernel, n_kv=n_kv, group=group, head_dim=head_dim,
                          q_blocks=q_blocks),
        grid=(batch, steps),
        in_specs=[
            pl.BlockSpec(memory_space=pltpu.SMEM),
            pl.BlockSpec((q_blocks * WINDOW, dq), qrow),
            pl.BlockSpec((s, kw), whole),
            pl.BlockSpec((s, kw), whole),
        ],
        out_specs=pl.BlockSpec((q_blocks * WINDOW, dq), qrow),
        out_shape=jax.ShapeDtypeStruct((t, dq), BF16),
        scratch_shapes=[
            pltpu.VMEM((inflight, n_kv, (group // 2) * WINDOW, 4 * WINDOW), BF16),
            pltpu.VMEM((inflight, n_heads // 2, WINDOW, 2 * head_dim), F32),
        ],
        compiler_params=_params(
            ("parallel", "arbitrary"),
            [((q_blocks * WINDOW, dq), BF16)] * 2 + [((s, kw), BF16)] * 2,
            [((inflight, n_heads, WINDOW, WINDOW), BF16),
             ((inflight, n_heads // 2, WINDOW, 2 * head_dim), F32)]),
        name="swa_attention",
    )(sinks, q, k2, v2)


def _tile(n, target):
    if n <= target:
        return n
    best = None
    for c in range(128, target + 1, 128):
        if n % c == 0:
            best = c
    assert best is not None, (n, target)
    return best


def kernel(x, norm1_g, norm2_g, ffn_w_in, ffn_w_out, lru_w_in, lru_conv_w, lru_conv_b, lru_w_rg,
           lru_b_rg, lru_w_ig, lru_b_ig, lru_lambda, lru_w_out, kv_norm_g, w_kv, k_norm_g, w_q,
           q_norm_g, sinks, w_o):
    batch, seq, d = x.shape
    t = batch * seq
    depth = norm1_g.shape[0]
    n_rec = lru_w_in.shape[0]
    head_dim = k_norm_g.shape[0]
    n_kv = w_kv.shape[1] // (2 * head_dim)
    n_heads = w_q.shape[2] // head_dim
    group = n_heads // n_kv
    assert seq % WINDOW == 0 and group % 2 == 0 and 2 * head_dim == 128
    assert MXU_COLS % head_dim == 0 and n_kv % 2 == 0

    tm = _tile(t, 1024)
    f = ffn_w_out.shape[1]
    tf = _tile(f, 512)
    tc = _tile(seq, 256)

    row = lambda v: v.reshape(1, -1).astype(F32)
    gidx = jnp.arange(MXU_COLS) // head_dim
    gsum = (gidx[:, None] == gidx[None, :]).astype(BF16)

    lru_w_in16 = lru_w_in.astype(BF16)
    lru_w_rg16, lru_w_ig16 = lru_w_rg.astype(BF16), lru_w_ig.astype(BF16)
    chain = _ffn_cast_ok(t, d, f, tm, tf)
    late = {"lru_w_out": lru_w_out, "w_q": w_q, "w_o": w_o}
    if chain:
        late.update(ffn_w_in=ffn_w_in, ffn_w_out=ffn_w_out)
    in_scan = {k: v for k, v in late.items()
               if n_rec > 0 and _cast_split(v.shape[1], v.shape[2], batch, seq // tc)}
    w16 = {k: v.astype(BF16) for k, v in late.items() if k not in in_scan and not k.startswith("ffn")}
    if not chain:
        w16.update(ffn_w_in=ffn_w_in.astype(BF16), ffn_w_out=ffn_w_out.astype(BF16))
    elif "ffn_w_in" not in in_scan or "ffn_w_out" not in in_scan:
        in_scan.pop("ffn_w_in", None), in_scan.pop("ffn_w_out", None)
        w16.update(ffn_w_in=ffn_w_in[:1].astype(BF16), ffn_w_out=ffn_w_out[:1].astype(BF16))

    xs = x.reshape(t, d)
    k2 = v2 = None
    for layer in range(depth):
        if layer < n_rec:
            i = layer
            gy, xbr = _rec_in_proj(xs, row(norm1_g[layer]), lru_w_in16, i, tm=tm,
                                   tn=lru_w_in.shape[2] // 2)
            casts = [(v, 0, 1 if k.startswith("ffn") else v.shape[0])
                     for k, v in in_scan.items()] if layer == 0 else []
            outs = _rglru(xbr, gy, lru_conv_w[i], row(lru_conv_b[i]), lru_w_rg16, row(lru_b_rg[i]),
                          lru_w_ig16, row(lru_b_ig[i]), row(lru_lambda[i]), i, casts,
                          batch=batch, tc=tc, lane_chunk=_tile(d, 512))
            m = outs[0]
            if casts:
                w16.update(zip(in_scan, outs[1:]))
            xs = _proj_residual(m, w16["lru_w_out"], i, xs, tm=_tile(t, 512), tn=d)
        else:
            j = layer - n_rec
            kv = None
            if layer == n_rec:
                wk, wv = jnp.split(w_kv, 2, axis=-1)
                kv = (row(kv_norm_g), wk.astype(BF16), wv.astype(BF16), row(jnp.tile(k_norm_g, n_kv)))
            outs = _q_proj(xs, row(norm1_g[layer]), w16["w_q"], j, row(jnp.tile(q_norm_g[j], n_heads)),
                           gsum, kv, head_dim=head_dim, tm=tm)
            q = outs[0]
            if kv is not None:
                k2, v2 = outs[1:]
            o = _attention(sinks[j].astype(F32), q, k2, v2, batch=batch, n_kv=n_kv, group=group,
                           head_dim=head_dim, q_blocks=min(8, seq // WINDOW))
            xs = _proj_residual(o, w16["w_o"], j, xs, tm=_tile(t, 512), tn=d)
        if chain:
            nxt = (ffn_w_in, ffn_w_out, layer + 1) if layer + 1 < depth else None
            outs = _ffn(xs, row(norm2_g[layer]), w16["ffn_w_in"], w16["ffn_w_out"], 0, nxt,
                        tm=tm, tf=tf)
            xs = outs[0]
            if nxt is not None:
                w16.update(ffn_w_in=outs[1], ffn_w_out=outs[2])
        else:
            xs = _ffn(xs, row(norm2_g[layer]), w16["ffn_w_in"], w16["ffn_w_out"], layer,
                      tm=tm, tf=tf)[0]
    return xs.reshape(batch, seq, d)
```

```python
import functools
import math

import jax
import jax.numpy as jnp
from jax import lax
from jax.experimental import pallas as pl
from jax.experimental.pallas import tpu as pltpu

NORM_EPS = 1e-6
LRU_C = 8.0
WINDOW = 128
NEG_INF = -1e30
TINY_F32 = 1e-37
LOG2_E = 1.4426950408889634
MXU_COLS = 256
VMEM_CAP_BYTES = 58 * 1024 * 1024
COMPILER_TEMP_BYTES = 12 * 1024 * 1024

F32 = jnp.float32
BF16 = jnp.bfloat16


def _nbytes(shape, dtype):
    return math.prod(shape) * jnp.dtype(dtype).itemsize


def _params(sem, windows, scratch=()):
    need = sum(2 * _nbytes(*w) for w in windows) + sum(_nbytes(*b) for b in scratch)
    limit = min(need + COMPILER_TEMP_BYTES, VMEM_CAP_BYTES)
    return pltpu.CompilerParams(dimension_semantics=sem, vmem_limit_bytes=limit)


def _rms_rows(x, g):
    ms = jnp.mean(x * x, axis=-1, keepdims=True)
    return (x * lax.rsqrt(ms + NORM_EPS)) * g


def _gelu_tanh(x):
    c = math.sqrt(2.0 / math.pi)
    xh = 0.5 * x
    t = jnp.tanh(x * (c + (0.044715 * c) * (x * x)))
    return xh + xh * t


def _group_rms(q, gsum_ref, group):
    cols = q.shape[-1]
    chunk = min(cols, MXU_COLS)
    outs = []
    for c in range(cols // chunk):
        qc = q[:, c * chunk:(c + 1) * chunk]
        ssq = jnp.dot((qc * qc).astype(BF16), gsum_ref[:chunk, :chunk], preferred_element_type=F32)
        outs.append(qc * lax.rsqrt(ssq * (1.0 / group) + NORM_EPS))
    return outs[0] if len(outs) == 1 else jnp.concatenate(outs, axis=-1)


def _repeat_heads(a, head_dim):
    pair_w = 2 * head_dim
    lane = lax.broadcasted_iota(jnp.int32, (a.shape[0], pair_w), 1)
    lo = lane < head_dim
    outs = []
    for c in range(a.shape[1] // pair_w):
        slab = a[:, c * pair_w:(c + 1) * pair_w]
        swapped = pltpu.roll(slab, head_dim, axis=1)
        outs += [jnp.where(lo, slab, swapped), jnp.where(lo, swapped, slab)]
    return jnp.concatenate(outs, axis=1)


def _rec_in_kernel(x_ref, g_ref, wy_ref, wx_ref, oy_ref, ox_ref, *scratch, one_step):
    if one_step:
        h = _rms_rows(x_ref[...], g_ref[...]).astype(BF16)
    else:
        (h_ref,) = scratch

        @pl.when(pl.program_id(1) == 0)
        def _():
            h_ref[...] = _rms_rows(x_ref[...], g_ref[...]).astype(BF16)

        h = h_ref[...]
    y = jnp.dot(h, wy_ref[...], preferred_element_type=F32)
    oy_ref[...] = _gelu_tanh(y).astype(BF16)
    ox_ref[...] = jnp.dot(h, wx_ref[...], preferred_element_type=F32).astype(BF16)


def _rec_in_proj(x, g, w_in, layer, *, tm, tn):
    t, d = x.shape
    w = w_in.shape[2] // 2
    nj = w // tn
    h_scratch = [] if nj == 1 else [((tm, d), BF16)]
    return pl.pallas_call(
        functools.partial(_rec_in_kernel, one_step=nj == 1),
        grid=(t // tm, nj),
        in_specs=[
            pl.BlockSpec((tm, d), lambda i, j: (i, 0)),
            pl.BlockSpec((1, d), lambda i, j: (0, 0)),
            pl.BlockSpec((None, d, tn), lambda i, j: (layer, 0, j)),
            pl.BlockSpec((None, d, tn), lambda i, j: (layer, 0, j + nj)),
        ],
        out_specs=[
            pl.BlockSpec((tm, tn), lambda i, j: (i, j)),
            pl.BlockSpec((tm, tn), lambda i, j: (i, j)),
        ],
        out_shape=[jax.ShapeDtypeStruct((t, w), BF16), jax.ShapeDtypeStruct((t, w), BF16)],
        scratch_shapes=[pltpu.VMEM(*b) for b in h_scratch],
        compiler_params=_params(
            ("parallel", "arbitrary"),
            [((tm, d), F32), ((d, tn), BF16), ((d, tn), BF16), ((tm, tn), BF16), ((tm, tn), BF16)],
            h_scratch),
        name="rec_in_proj",
    )(x, g, w_in, w_in)


def _rglru_kernel(xbr_ref, gy_ref, perm_ref, permt_ref, cw_ref, cb_ref, wrg_ref, brg_ref, wig_ref,
                  big_ref, lam_ref, *rest, lane_chunk, n_cast):
    o_ref = rest[n_cast]
    for src_ref, dst_ref in zip(rest[:n_cast], rest[n_cast + 1:2 * n_cast + 1]):
        dst_ref[...] = src_ref[...].astype(BF16)
    tail_ref, carry_ref, a_ref, u_ref, gp_ref, m_ref = rest[2 * n_cast + 1:]
    tc, d = xbr_ref.shape
    taps = cw_ref.shape[0]
    seg = tc // 8
    halo = taps - 1

    @pl.when(pl.program_id(1) == 0)
    def _():
        tail_ref[...] = jnp.zeros_like(tail_ref)
        carry_ref[...] = jnp.zeros_like(carry_ref)

    perm = perm_ref[...]
    x0 = jnp.dot(perm, xbr_ref[...], preferred_element_type=F32)
    gp_ref[...] = jnp.dot(perm, gy_ref[...], preferred_element_type=F32)

    first_seg = lax.broadcasted_iota(jnp.int32, (8, d), 0) == 0
    tail = tail_ref[...]
    wrapped = []
    for i in range(halo):
        cur = x0[(seg - halo + i) * 8:(seg - halo + i + 1) * 8]
        prv = tail[i * 8:(i + 1) * 8]
        wrapped.append(jnp.where(first_seg, pltpu.roll(prv, 1, axis=0), pltpu.roll(cur, 1, axis=0)))
    tail_ref[...] = x0[(seg - halo) * 8:]

    cw_half = 0.5 * cw_ref[...]
    xh = x0 * cw_half[taps - 1:taps, :] + 0.5 * cb_ref[...]
    for k in range(1, taps):
        xk = jnp.concatenate(wrapped[halo - k:] + [x0[:tc - 8 * k]], axis=0)
        xh = xh + xk * cw_half[taps - 1 - k:taps - k, :]

    xh16 = xh.astype(BF16)
    nblk = wrg_ref.shape[0]
    bw = d // nblk
    for n in range(nblk):
        sl = slice(n * bw, (n + 1) * bw)
        xs = xh16[:, sl]
        tr = jnp.tanh(jnp.dot(xs, wrg_ref[n], preferred_element_type=F32) + 0.5 * brg_ref[:, sl])
        ti = jnp.tanh(jnp.dot(xs, wig_ref[n], preferred_element_type=F32) + 0.5 * big_ref[:, sl])
        nlam = -lam_ref[:, sl]
        softplus = jnp.maximum(nlam, 0.0) + jnp.log1p(jnp.exp(-jnp.abs(nlam)))
        half_c = (0.5 * LRU_C) * softplus
        z = tr * half_c + half_c
        a = jnp.exp2(z * (-LOG2_E))
        one_minus_a2 = jnp.tanh(z) * (1.0 + a * a)
        root = one_minus_a2 * lax.rsqrt(jnp.maximum(one_minus_a2, TINY_F32))
        a_ref[:, sl] = a
        u_ref[:, sl] = (root * xh[:, sl]) * (ti + 1.0)

    rowc = lax.broadcasted_iota(jnp.int32, (8, lane_chunk), 0)
    for c in range(d // lane_chunk):
        ls = slice(c * lane_chunk, (c + 1) * lane_chunk)

        h_end = u_ref[0:8, ls]
        e_end = a_ref[0:8, ls]
        for j in range(1, seg):
            av = a_ref[j * 8:(j + 1) * 8, ls]
            h_end = av * h_end + u_ref[j * 8:(j + 1) * 8, ls]
            e_end = av * e_end
            u_ref[j * 8:(j + 1) * 8, ls] = h_end
            a_ref[j * 8:(j + 1) * 8, ls] = e_end

        for s in (1, 2, 4):
            keep = rowc >= s
            e_prev = jnp.where(keep, pltpu.roll(e_end, s, axis=0), 1.0)
            h_prev = jnp.where(keep, pltpu.roll(h_end, s, axis=0), 0.0)
            h_end = e_end * h_prev + h_end
            e_end = e_end * e_prev
        state_in = carry_ref[:, ls]
        after = e_end * state_in + h_end
        seg_in = jnp.where(rowc == 0, state_in, pltpu.roll(after, 1, axis=0))
        carry_ref[:, ls] = jnp.broadcast_to(after[7:8, :], after.shape)
        seg_in2 = jnp.concatenate([seg_in, seg_in], axis=0)

        for jj in range(seg // 2):
            rs = slice(jj * 16, (jj + 1) * 16)
            h = u_ref[rs, ls] + a_ref[rs, ls] * seg_in2
            m_ref[rs, ls] = (h * gp_ref[rs, ls]).astype(BF16)

    o_ref[...] = jnp.dot(permt_ref[...], m_ref[...], preferred_element_type=F32).astype(BF16)


def _cast_split(rows, cols, batch, nt):
    if rows % nt == 0 and (rows // nt) % 16 == 0 and cols % batch == 0 and (cols // batch) % 128 == 0:
        return (rows // nt, cols // batch), lambda b, i: (i, b)
    if rows % batch == 0 and (rows // batch) % 16 == 0 and cols % nt == 0 and (cols // nt) % 128 == 0:
        return (rows // batch, cols // nt), lambda b, i: (b, i)
    return None


def _rglru(xbr, gy, conv_w, conv_b, w_rg, b_rg, w_ig, b_ig, lam, layer, casts=(), *,
           batch, tc, lane_chunk):
    t, d = xbr.shape
    s = t // batch
    nt = s // tc
    _, nblk, bw, _ = w_rg.shape
    taps = conv_w.shape[0]
    assert tc % 16 == 0 and taps - 1 <= tc // 8
    rows = jnp.arange(tc)
    src = (rows % 8) * (tc // 8) + rows // 8
    perm = (src[:, None] == rows[None, :]).astype(BF16)
    row = lambda b, i: (b * nt + i, 0)
    fixed2 = lambda b, i: (0, 0)
    fixed4 = lambda b, i: (layer, 0, 0, 0)
    in_specs = [
        pl.BlockSpec((tc, d), row),
        pl.BlockSpec((tc, d), row),
        pl.BlockSpec((tc, tc), fixed2),
        pl.BlockSpec((tc, tc), fixed2),
        pl.BlockSpec(conv_w.shape, fixed2),
        pl.BlockSpec((1, d), fixed2),
        pl.BlockSpec((None, nblk, bw, bw), fixed4),
        pl.BlockSpec((1, d), fixed2),
        pl.BlockSpec((None, nblk, bw, bw), fixed4),
        pl.BlockSpec((1, d), fixed2),
        pl.BlockSpec((1, d), fixed2),
    ]
    out_specs = [pl.BlockSpec((tc, d), row)]
    out_shape = [jax.ShapeDtypeStruct((t, d), BF16)]
    args = [xbr, gy, perm, perm.T, conv_w, conv_b, w_rg, b_rg, w_ig, b_ig, lam]
    windows = [((tc, d), BF16)] * 3 + [((tc, tc), BF16)] * 2 + [((nblk, bw, bw), BF16)] * 2
    for arr, first, count in casts:
        blk, where = _cast_split(arr.shape[1], arr.shape[2], batch, nt)
        assert first % count == 0
        in_specs.append(pl.BlockSpec((count,) + blk, lambda b, i, w=where, l=first // count: (l,) + w(b, i)))
    for arr, first, count in casts:
        blk, where = _cast_split(arr.shape[1], arr.shape[2], batch, nt)
        out_specs.append(pl.BlockSpec((count,) + blk, lambda b, i, w=where: (0,) + w(b, i)))
        out_shape.append(jax.ShapeDtypeStruct((count,) + arr.shape[1:], BF16))
        args.append(arr)
        windows += [((count,) + blk, F32), ((count,) + blk, BF16)]
    return pl.pallas_call(
        functools.partial(_rglru_kernel, lane_chunk=lane_chunk, n_cast=len(casts)),
        grid=(batch, nt),
        in_specs=in_specs,
        out_specs=out_specs,
        out_shape=out_shape,
        scratch_shapes=[
            pltpu.VMEM(((taps - 1) * 8, d), F32),
            pltpu.VMEM((8, d), F32),
            pltpu.VMEM((tc, d), F32),
            pltpu.VMEM((tc, d), F32),
            pltpu.VMEM((tc, d), F32),
            pltpu.VMEM((tc, d), BF16),
        ],
        compiler_params=_params(("parallel", "arbitrary"), windows,
                                [((tc, d), F32)] * 3 + [((tc, d), BF16)]),
        name="rglru_scan",
    )(*args)


def _proj_res_kernel(m_ref, w_ref, x_ref, o_ref):
    o_ref[...] = x_ref[...] + jnp.dot(m_ref[...], w_ref[...], preferred_element_type=F32)


def _proj_residual(m, w, layer, x, *, tm, tn):
    t, k = m.shape
    n = w.shape[2]
    return pl.pallas_call(
        _proj_res_kernel,
        grid=(t // tm, n // tn),
        in_specs=[
            pl.BlockSpec((tm, k), lambda i, j: (i, 0)),
            pl.BlockSpec((None, k, tn), lambda i, j: (layer, 0, j)),
            pl.BlockSpec((tm, tn), lambda i, j: (i, j)),
        ],
        out_specs=pl.BlockSpec((tm, tn), lambda i, j: (i, j)),
        out_shape=jax.ShapeDtypeStruct((t, n), F32),
        compiler_params=_params(
            ("parallel", "arbitrary"),
            [((tm, k), BF16), ((k, tn), BF16), ((tm, tn), F32), ((tm, tn), F32)]),
        name="proj_residual",
    )(m, w, x)


def _ffn_kernel(x_ref, g_ref, wg_ref, wu_ref, wo_ref, *rest, cast_next):
    if cast_next:
        nwi_ref, nwo_ref, o_ref, cwi_ref, cwo_ref, h_ref = rest
        cwi_ref[...] = nwi_ref[...].astype(BF16)
        cwo_ref[...] = nwo_ref[...].astype(BF16)
    else:
        o_ref, h_ref = rest

    def hidden_chunk(h):
        gate = jnp.dot(h, wg_ref[...], preferred_element_type=F32)
        up = jnp.dot(h, wu_ref[...], preferred_element_type=F32)
        act = ((gate * jax.nn.sigmoid(gate)) * up).astype(BF16)
        return jnp.dot(act, wo_ref[...], preferred_element_type=F32)

    @pl.when(pl.program_id(1) == 0)
    def _():
        x = x_ref[...]
        h = _rms_rows(x, g_ref[...]).astype(BF16)
        h_ref[...] = h
        o_ref[...] = x + hidden_chunk(h)

    @pl.when(pl.program_id(1) > 0)
    def _():
        o_ref[...] += hidden_chunk(h_ref[...])


def _ffn_cast_ok(t, d, f, tm, tf):
    ni, nf = t // tm, f // tf
    return (d % ni == 0 and (d // ni) % 128 == 0 and (2 * f) % nf == 0
            and (2 * f // nf) % 128 == 0)


def _ffn(x, g, w_in, w_out, layer, next_w=None, *, tm, tf):
    t, d = x.shape
    f = w_out.shape[1]
    ni, nf = t // tm, f // tf
    in_specs = [
        pl.BlockSpec((tm, d), lambda i, j: (i, 0)),
        pl.BlockSpec((1, d), lambda i, j: (0, 0)),
        pl.BlockSpec((None, d, tf), lambda i, j: (layer, 0, j)),
        pl.BlockSpec((None, d, tf), lambda i, j: (layer, 0, j + nf)),
        pl.BlockSpec((None, tf, d), lambda i, j: (layer, j, 0)),
    ]
    out_specs = [pl.BlockSpec((tm, d), lambda i, j: (i, 0))]
    out_shape = [jax.ShapeDtypeStruct((t, d), F32)]
    args = [x, g, w_in, w_in, w_out]
    windows = [((tm, d), F32)] * 2 + [((d, tf), BF16)] * 3
    if next_w is not None:
        nwi, nwo, nl = next_w
        ri, ci, co = d // ni, 2 * f // nf, d // ni
        in_specs += [pl.BlockSpec((None, ri, ci), lambda i, j: (nl, i, j)),
                     pl.BlockSpec((None, tf, co), lambda i, j: (nl, j, i))]
        out_specs += [pl.BlockSpec((None, ri, ci), lambda i, j: (0, i, j)),
                      pl.BlockSpec((None, tf, co), lambda i, j: (0, j, i))]
        out_shape += [jax.ShapeDtypeStruct((1, d, 2 * f), BF16), jax.ShapeDtypeStruct((1, f, d), BF16)]
        args += [nwi, nwo]
        windows += [((ri, ci), F32), ((tf, co), F32), ((ri, ci), BF16), ((tf, co), BF16)]
    return pl.pallas_call(
        functools.partial(_ffn_kernel, cast_next=next_w is not None),
        grid=(ni, nf),
        in_specs=in_specs,
        out_specs=out_specs,
        out_shape=out_shape,
        scratch_shapes=[pltpu.VMEM((tm, d), BF16)],
        compiler_params=_params(("parallel", "arbitrary"), windows, [((tm, d), BF16)]),
        name="swiglu_ffn",
    )(*args)


def _q_kernel(x_ref, g_ref, w_ref, qg_ref, gsum_ref, *rest, head_dim, with_kv):
    x = x_ref[...]
    xn = x * lax.rsqrt(jnp.mean(x * x, axis=-1, keepdims=True) + NORM_EPS)
    if with_kv:
        gkv_ref, wk_ref, wv_ref, kg_ref, o_ref, k_ref, v_ref = rest
        hkv = (xn * gkv_ref[...]).astype(BF16)
        k = jnp.dot(hkv, wk_ref[...], preferred_element_type=F32)
        kn = _group_rms(k, gsum_ref, head_dim) * kg_ref[...]
        k_ref[...] = _repeat_heads(kn, head_dim).astype(BF16)
        v = jnp.dot(hkv, wv_ref[...], preferred_element_type=F32)
        v_ref[...] = _repeat_heads(v, head_dim).astype(BF16)
    else:
        (o_ref,) = rest
    q = jnp.dot((xn * g_ref[...]).astype(BF16), w_ref[...], preferred_element_type=F32)
    qn = _group_rms(q, gsum_ref, head_dim) * qg_ref[...]
    o_ref[...] = (qn * (1.0 / math.sqrt(head_dim))).astype(BF16)


def _q_proj(x, g, w, layer, qg_t, gsum, kv=None, *, head_dim, tm):
    t, d = x.shape
    n = w.shape[2]
    fixed = lambda i: (0, 0)
    in_specs = [
        pl.BlockSpec((tm, d), lambda i: (i, 0)),
        pl.BlockSpec((1, d), fixed),
        pl.BlockSpec((None, d, n), lambda i: (layer, 0, 0)),
        pl.BlockSpec((1, n), fixed),
        pl.BlockSpec((MXU_COLS, MXU_COLS), fixed),
    ]
    out_specs = [pl.BlockSpec((tm, n), lambda i: (i, 0))]
    out_shape = [jax.ShapeDtypeStruct((t, n), BF16)]
    args = [x, g, w, qg_t, gsum]
    windows = [((tm, d), F32), ((d, n), BF16), ((tm, n), BF16)]
    if kv is not None:
        nk = kv[1].shape[1]
        in_specs += [pl.BlockSpec((1, d), fixed), pl.BlockSpec((d, nk), fixed),
                     pl.BlockSpec((d, nk), fixed), pl.BlockSpec((1, nk), fixed)]
        out_specs += [pl.BlockSpec((tm, 2 * nk), lambda i: (i, 0))] * 2
        out_shape += [jax.ShapeDtypeStruct((t, 2 * nk), BF16)] * 2
        args += list(kv)
        windows += [((d, nk), BF16)] * 2 + [((tm, 2 * nk), BF16)] * 2
    return pl.pallas_call(
        functools.partial(_q_kernel, head_dim=head_dim, with_kv=kv is not None),
        grid=(t // tm,),
        in_specs=in_specs,
        out_specs=out_specs,
        out_shape=out_shape,
        compiler_params=_params(("parallel",), windows),
        name="q_proj",
    )(*args)


def _attn_kernel(sink_ref, q_ref, k_ref, v_ref, o_ref, p_ref, es_ref, *,
                 n_kv, group, head_dim, q_blocks):
    blk = WINDOW
    inflight = p_ref.shape[0]
    pairs = group // 2
    pair_w = 2 * head_dim
    lane = lax.broadcasted_iota(jnp.int32, (blk, pair_w), 1)
    lo_half = lane < head_dim
    row = lax.broadcasted_iota(jnp.int32, (blk, blk), 0)
    col = lax.broadcasted_iota(jnp.int32, (blk, blk), 1)
    from_cur = col <= row
    cur16 = jnp.where(from_cur, 1.0, 0.0).astype(BF16)
    prev16 = jnp.where(from_cur, 0.0, 1.0).astype(BF16)
    lo_half32 = lane.astype(F32) < float(head_dim)
    zero = jnp.zeros((), BF16)

    def halves(prev, cur):
        return jnp.concatenate([
            jnp.where(lo_half, prev, zero), jnp.where(lo_half, cur, zero),
            jnp.where(lo_half, zero, prev), jnp.where(lo_half, zero, cur)], axis=0)

    lo16 = jnp.where(lo_half32, 1.0, 0.0).astype(BF16)
    hi16 = jnp.where(lo_half32, 0.0, 1.0).astype(BF16)
    ones_cat = jnp.concatenate([lo16, lo16, hi16, hi16], axis=0)

    def phase1(sub, n, r_q):
        r_cur = pl.multiple_of(n * blk, blk)
        r_prev = pl.multiple_of(jnp.maximum(n - 1, 0) * blk, blk)
        prev_bias = jnp.where(n > 0, 0.0, NEG_INF)
        for h in range(n_kv):
            hs = slice(h * pair_w, (h + 1) * pair_w)
            kcat = halves(k_ref[pl.ds(r_prev, blk), hs], k_ref[pl.ds(r_cur, blk), hs])
            qstack = jnp.concatenate(
                [q_ref[pl.ds(r_q, blk), (h * pairs + p) * pair_w:(h * pairs + p + 1) * pair_w]
                 for p in range(pairs)], axis=0)
            s = lax.dot_general(qstack, kcat, (((1,), (1,)), ((), ())),
                                preferred_element_type=F32)
            for p in range(pairs):
                ms = []
                for e in range(2):
                    s_prev = s[p * blk:(p + 1) * blk, (2 * e) * blk:(2 * e + 1) * blk] + prev_bias
                    s_cur = s[p * blk:(p + 1) * blk, (2 * e + 1) * blk:(2 * e + 2) * blk]
                    sc = jnp.where(from_cur, s_cur, s_prev)
                    m = jnp.max(sc, axis=-1, keepdims=True)
                    pexp = jnp.exp(sc - m).astype(BF16)
                    p_ref[sub, h, p * blk:(p + 1) * blk, (2 * e) * blk:(2 * e + 1) * blk] = (
                        pexp * prev16)
                    p_ref[sub, h, p * blk:(p + 1) * blk, (2 * e + 1) * blk:(2 * e + 2) * blk] = (
                        pexp * cur16)
                    ms.append(sink_ref[h * group + 2 * p + e] - m)
                es_ref[sub, h * pairs + p] = jnp.exp(jnp.where(lo_half32, ms[0], ms[1]))

    def phase3(sub, n, r_q):
        r_cur = pl.multiple_of(n * blk, blk)
        r_prev = pl.multiple_of(jnp.maximum(n - 1, 0) * blk, blk)
        for h in range(n_kv):
            hs = slice(h * pair_w, (h + 1) * pair_w)
            vcat = halves(v_ref[pl.ds(r_prev, blk), hs], v_ref[pl.ds(r_cur, blk), hs])
            pv = jnp.dot(p_ref[sub, h], jnp.concatenate([vcat, ones_cat], axis=1),
                         preferred_element_type=F32)
            for p in range(pairs):
                num = pv[p * blk:(p + 1) * blk, :pair_w]
                den = pv[p * blk:(p + 1) * blk, pair_w:] + es_ref[sub, h * pairs + p]
                o_ref[pl.ds(r_q, blk), (h * pairs + p) * pair_w:(h * pairs + p + 1) * pair_w] = (
                    num * (1.0 / den)).astype(BF16)

    def block_body(it, _):
        blocks = []
        for sub in range(inflight):
            qb = it * inflight + sub
            n = pl.program_id(1) * q_blocks + qb
            blocks.append((sub, n, pl.multiple_of(qb * blk, blk)))
        for sub, n, r_q in blocks:
            phase1(sub, n, r_q)
        for sub, n, r_q in blocks:
            phase3(sub, n, r_q)
        return 0

    lax.fori_loop(0, q_blocks // inflight, block_body, 0)


def _attention(sinks, q, k2, v2, *, batch, n_kv, group, head_dim, q_blocks):
    t, dq = q.shape
    s = t // batch
    nb = s // WINDOW
    inflight = 4 if q_blocks % 4 == 0 else 1
    assert nb % q_blocks == 0
    kw = k2.shape[1]
    n_heads = n_kv * group
    steps = nb // q_blocks
    qrow = lambda b, i: (b * steps + i, 0)
    whole = lambda b, i: (b, 0)
    return pl.pallas_call(
        functools.partial(_attn_kernel, n_kv=n_kv, group=group, head_dim=head_dim,
                          q_blocks=q_blocks),
        grid=(batch, steps),
        in_specs=[
            pl.BlockSpec(memory_space=pltpu.SMEM),
            pl.BlockSpec((q_blocks * WINDOW, dq), qrow),
            pl.BlockSpec((s, kw), whole),
            pl.BlockSpec((s, kw), whole),
        ],
        out_specs=pl.BlockSpec((q_blocks * WINDOW, dq), qrow),
        out_shape=jax.ShapeDtypeStruct((t, dq), BF16),
        scratch_shapes=[
            pltpu.VMEM((inflight, n_kv, (group // 2) * WINDOW, 4 * WINDOW), BF16),
            pltpu.VMEM((inflight, n_heads // 2, WINDOW, 2 * head_dim), F32),
        ],
        compiler_params=_params(
            ("parallel", "arbitrary"),
            [((q_blocks * WINDOW, dq), BF16)] * 2 + [((s, kw), BF16)] * 2,
            [((inflight, n_heads, WINDOW, WINDOW), BF16),
             ((inflight, n_heads // 2, WINDOW, 2 * head_dim), F32)]),
        name="swa_attention",
    )(sinks, q, k2, v2)


def _tile(n, target):
    if n <= target:
        return n
    best = None
    for c in range(128, target + 1, 128):
        if n % c == 0:
            best = c
    assert best is not None, (n, target)
    return best


def kernel(x, norm1_g, norm2_g, ffn_w_in, ffn_w_out, lru_w_in, lru_conv_w, lru_conv_b, lru_w_rg,
           lru_b_rg, lru_w_ig, lru_b_ig, lru_lambda, lru_w_out, kv_norm_g, w_kv, k_norm_g, w_q,
           q_norm_g, sinks, w_o):
    batch, seq, d = x.shape
    t = batch * seq
    depth = norm1_g.shape[0]
    n_rec = lru_w_in.shape[0]
    head_dim = k_norm_g.shape[0]
    n_kv = w_kv.shape[1] // (2 * head_dim)
    n_heads = w_q.shape[2] // head_dim
    group = n_heads // n_kv
    assert seq % WINDOW == 0 and group % 2 == 0 and 2 * head_dim == 128
    assert MXU_COLS % head_dim == 0 and n_kv % 2 == 0

    tm = _tile(t, 1024)
    f = ffn_w_out.shape[1]
    tf = _tile(f, 512)
    tc = _tile(seq, 256)

    row = lambda v: v.reshape(1, -1).astype(F32)
    gidx = jnp.arange(MXU_COLS) // head_dim
    gsum = (gidx[:, None] == gidx[None, :]).astype(BF16)

    lru_w_in16 = lru_w_in.astype(BF16)
    lru_w_rg16, lru_w_ig16 = lru_w_rg.astype(BF16), lru_w_ig.astype(BF16)
    chain = _ffn_cast_ok(t, d, f, tm, tf)
    late = {"lru_w_out": lru_w_out, "w_q": w_q, "w_o": w_o}
    if chain:
        late.update(ffn_w_in=ffn_w_in, ffn_w_out=ffn_w_out)
    in_scan = {k: v for k, v in late.items()
               if n_rec > 0 and _cast_split(v.shape[1], v.shape[2], batch, seq // tc)}
    w16 = {k: v.astype(BF16) for k, v in late.items() if k not in in_scan and not k.startswith("ffn")}
    if not chain:
        w16.update(ffn_w_in=ffn_w_in.astype(BF16), ffn_w_out=ffn_w_out.astype(BF16))
    elif "ffn_w_in" not in in_scan or "ffn_w_out" not in in_scan:
        in_scan.pop("ffn_w_in", None), in_scan.pop("ffn_w_out", None)
        w16.update(ffn_w_in=ffn_w_in[:1].astype(BF16), ffn_w_out=ffn_w_out[:1].astype(BF16))

    xs = x.reshape(t, d)
    k2 = v2 = None
    for layer in range(depth):
        if layer < n_rec:
            i = layer
            gy, xbr = _rec_in_proj(xs, row(norm1_g[layer]), lru_w_in16, i, tm=tm,
                                   tn=lru_w_in.shape[2] // 2)
            casts = [(v, 0, 1 if k.startswith("ffn") else v.shape[0])
                     for k, v in in_scan.items()] if layer == 0 else []
            outs = _rglru(xbr, gy, lru_conv_w[i], row(lru_conv_b[i]), lru_w_rg16, row(lru_b_rg[i]),
                          lru_w_ig16, row(lru_b_ig[i]), row(lru_lambda[i]), i, casts,
                          batch=batch, tc=tc, lane_chunk=_tile(d, 512))
            m = outs[0]
            if casts:
                w16.update(zip(in_scan, outs[1:]))
            xs = _proj_residual(m, w16["lru_w_out"], i, xs, tm=_tile(t, 512), tn=d)
        else:
            j = layer - n_rec
            kv = None
            if layer == n_rec:
                wk, wv = jnp.split(w_kv, 2, axis=-1)
                kv = (row(kv_norm_g), wk.astype(BF16), wv.astype(BF16), row(jnp.tile(k_norm_g, n_kv)))
            outs = _q_proj(xs, row(norm1_g[layer]), w16["w_q"], j, row(jnp.tile(q_norm_g[j], n_heads)),
                           gsum, kv, head_dim=head_dim, tm=tm)
            q = outs[0]
            if kv is not None:
                k2, v2 = outs[1:]
            o = _attention(sinks[j].astype(F32), q, k2, v2, batch=batch, n_kv=n_kv, group=group,
                           head_dim=head_dim, q_blocks=min(8, seq // WINDOW))
            xs = _proj_residual(o, w16["w_o"], j, xs, tm=_tile(t, 512), tn=d)
        if chain:
            nxt = (ffn_w_in, ffn_w_out, layer + 1) if layer + 1 < depth else None
            outs = _ffn(xs, row(norm2_g[layer]), w16["ffn_w_in"], w16["ffn_w_out"], 0, nxt,
                        tm=tm, tf=tf)
            xs = outs[0]
            if nxt is not None:
                w16.update(ffn_w_in=outs[1], ffn_w_out=outs[2])
        else:
            xs = _ffn(xs, row(norm2_g[layer]), w16["ffn_w_in"], w16["ffn_w_out"], layer,
                      tm=tm, tf=tf)[0]
    return xs.reshape(batch, seq, d)
```
